```python
import math
import jax, jax.numpy as jnp
from jax import lax
import numpy as np

D_MODEL = 2048
BATCH = 1
SEQ = 8192
DEPTH = 2

HEAD_DIM = 128
N_MIX_HEADS = D_MODEL // HEAD_DIM
DIFF_HEADS = N_MIX_HEADS // 4
MOBA_HEADS = (N_MIX_HEADS - DIFF_HEADS) // 2
DSA_HEADS = N_MIX_HEADS - DIFF_HEADS - MOBA_HEADS
MOBA_BLOCK = 256
MOBA_TOPK = 3
MOBA_Q_BLOCK = 64
DIFF_QK_DIM = HEAD_DIM // 2
DIFF_V_DIM = HEAD_DIM
DSA_TOPK = 256
IDX_HEADS = 8
IDX_DIM = 64
Q_BLOCK = 128
MEM_LEN = 256
CROSS_HEADS = 4
D_FF = 5504
CONV_WIDTH = 3
ROPE_THETA = 10000.0
RMS_EPS = 1e-6

MOBA_W = MOBA_HEADS * HEAD_DIM
DIFF_QK_W = DIFF_HEADS * 2 * DIFF_QK_DIM
DIFF_V_W = DIFF_HEADS * DIFF_V_DIM
DSA_W = DSA_HEADS * HEAD_DIM
IDX_Q_W = IDX_HEADS * IDX_DIM
D_MIX = MOBA_W + DIFF_V_W + DSA_W
IN_SPLITS = (MOBA_W, MOBA_W, MOBA_W, DIFF_QK_W, DIFF_QK_W, DIFF_V_W,
             DSA_W, DSA_W, DSA_W, IDX_Q_W, IDX_DIM, IDX_HEADS)
D_IN = sum(IN_SPLITS)
CROSS_W = CROSS_HEADS * HEAD_DIM

kernel_name = 'hymba_style_moba_diff_dsa_hybrid'

F32 = jnp.float32


def rms_norm(x, g):
    xf = x.astype(F32)
    y = xf * lax.rsqrt(jnp.mean(xf * xf, axis=-1, keepdims=True) + RMS_EPS)
    return (y * g.astype(F32)).astype(x.dtype)


def rope(x, pos):
    d = x.shape[-1]
    half = d // 2
    inv = ROPE_THETA ** (-(jnp.arange(half, dtype=F32) * 2.0 / d))
    ang = pos.astype(F32)[..., None] * inv
    cos = jnp.cos(ang)[:, :, None, :]
    sin = jnp.sin(ang)[:, :, None, :]
    xf = x.astype(F32)
    x1, x2 = xf[..., :half], xf[..., half:]
    return jnp.concatenate([x1 * cos - x2 * sin, x2 * cos + x1 * sin], axis=-1).astype(x.dtype)


def _split_columns(proj):
    offs, acc = [], 0
    for w in IN_SPLITS[:-1]:
        acc += w
        offs.append(acc)
    return jnp.split(proj, offs, axis=-1)


def _query_block_sweep(fn, block, *arrays):
    b, s = arrays[0].shape[:2]
    n = s // block
    blocked = tuple(a.reshape(b, n, block, *a.shape[2:]).swapaxes(0, 1) for a in arrays)
    out = lax.map(lambda args: fn(args[0], *args[1:]), (jnp.arange(n, dtype=jnp.int32),) + blocked)
    return out.swapaxes(0, 1).reshape(b, s, *out.shape[3:])


def moba_attention(q, k, v):
    b, s, h, d = q.shape
    nb = -(-s // MOBA_BLOCK)
    pad = nb * MOBA_BLOCK - s
    kp = jnp.pad(k, ((0, 0), (0, pad), (0, 0), (0, 0)))
    vp = jnp.pad(v, ((0, 0), (0, pad), (0, 0), (0, 0)))
    kb = kp.reshape(b, nb, MOBA_BLOCK, h, d)
    cnt = jnp.clip(s - jnp.arange(nb) * MOBA_BLOCK, 1, MOBA_BLOCK).astype(F32)
    kbar = kb.astype(F32).sum(axis=2) / cnt[None, :, None, None]
    kb_t = kb.transpose(0, 3, 1, 2, 4)
    vb_t = vp.reshape(b, nb, MOBA_BLOCK, h, d).transpose(0, 3, 1, 2, 4)
    n_sel = max(1, min(MOBA_TOPK, nb - 1))
    n_s = n_sel * MOBA_BLOCK
    scale = d ** -0.5
    bi = jnp.arange(b)[:, None, None, None]
    hi = jnp.arange(h)[None, None, :, None]

    def step(qi, qblk):
        t = qi * MOBA_Q_BLOCK + jnp.arange(MOBA_Q_BLOCK)
        cur = (qi * MOBA_Q_BLOCK) // MOBA_BLOCK
        qf = qblk.astype(F32)
        gate = jnp.einsum('bqhd,bnhd->bqhn', qf, kbar)
        gate = jnp.where(jnp.arange(nb) < cur, gate, -jnp.inf)
        _, sel = lax.top_k(gate, n_sel)
        sel_ok = sel < cur
        ks = kb_t[bi, hi, sel].astype(F32)
        vs = vb_t[bi, hi, sel].astype(F32)
        s_sel = jnp.einsum('bqhd,bqhnkd->bqhnk', qf, ks) * scale
        s_sel = jnp.where(sel_ok[..., None], s_sel, -jnp.inf).reshape(b, MOBA_Q_BLOCK, h, n_s)
        k_own = lax.dynamic_slice_in_dim(kp, cur * MOBA_BLOCK, MOBA_BLOCK, axis=1).astype(F32)
        v_own = lax.dynamic_slice_in_dim(vp, cur * MOBA_BLOCK, MOBA_BLOCK, axis=1).astype(F32)
        s_own = jnp.einsum('bqhd,bkhd->bqhk', qf, k_own) * scale
        causal = (cur * MOBA_BLOCK + jnp.arange(MOBA_BLOCK))[None, :] <= t[:, None]
        s_own = jnp.where(causal[None, :, None, :], s_own, -jnp.inf)
        p = jax.nn.softmax(jnp.concatenate([s_sel, s_own], axis=-1), axis=-1)
        p_sel = p[..., :n_s].reshape(b, MOBA_Q_BLOCK, h, n_sel, MOBA_BLOCK)
        o = (jnp.einsum('bqhnk,bqhnkd->bqhd', p_sel, vs)
             + jnp.einsum('bqhk,bkhd->bqhd', p[..., n_s:], v_own))
        return o.astype(qblk.dtype)

    return _query_block_sweep(step, MOBA_Q_BLOCK, q)


def diff_attention(q, k, v, lam):
    s = q.shape[1]
    scale = q.shape[-1] ** -0.5
    kf = k.astype(F32)
    vf = v.astype(F32)
    kpos = jnp.arange(s)

    def step(qi, qblk):
        t = qi * Q_BLOCK + jnp.arange(Q_BLOCK)
        sc = jnp.einsum('bqhcd,bshcd->bchqs', qblk.astype(F32), kf) * scale
        causal = kpos[None, :] <= t[:, None]
        sc = jnp.where(causal[None, None, None], sc, -jnp.inf)
        p = jax.nn.softmax(sc, axis=-1)
        a = p[:, 0] - lam * p[:, 1]
        return jnp.einsum('bhqs,bshd->bqhd', a, vf).astype(v.dtype)

    return _query_block_sweep(step, Q_BLOCK, q)


def dsa_attention(q, k, v, q_idx, k_idx, w_idx):
    b, s, h, d = q.shape
    n_keep = min(DSA_TOPK, s // 4)
    scale = d ** -0.5
    idx_scale = q_idx.shape[-1] ** -0.5
    kidx_f = k_idx.astype(F32)
    kpos = jnp.arange(s)
    bi = jnp.arange(b)[:, None, None]

    def step(qi, qblk, qiblk, wblk):
        t = qi * Q_BLOCK + jnp.arange(Q_BLOCK)
        rel = jax.nn.relu(jnp.einsum('bqhd,bsd->bqhs', qiblk.astype(F32), kidx_f) * idx_scale)
        score = jnp.einsum('bqhs,bqh->bqs', rel, wblk.astype(F32))
        admissible = kpos[None, :] <= t[:, None]
        score = jnp.where(admissible[None], score, -jnp.inf)
        _, sel = lax.top_k(score, n_keep)
        sel_ok = sel <= t[None, :, None]
        ks = k[bi, sel].astype(F32)
        vs = v[bi, sel].astype(F32)
        logits = jnp.einsum('bqhd,bqkhd->bqhk', qblk.astype(F32), ks) * scale
        logits = jnp.where(sel_ok[:, :, None, :], logits, -jnp.inf)
        p = jax.nn.softmax(logits, axis=-1)
        return jnp.einsum('bqhk,bqkhd->bqhd', p, vs).astype(qblk.dtype)

    return _query_block_sweep(step, Q_BLOCK, q, q_idx, w_idx)


def setup_inputs(seed: int = 0) -> dict:
    key = jax.random.key(seed)
    ks = jax.random.split(key, 24)

    def nrm(k, shape, scale):
        return jax.random.normal(k, shape, F32) * scale

    def gain(k, shape):
        return 1.0 + 0.02 * jax.random.normal(k, shape, F32)

    L = DEPTH
    x = nrm(ks[0], (BATCH, SEQ, D_MODEL), 1.0)
    mem = nrm(ks[1], (BATCH, MEM_LEN, D_MODEL), 1.0)
    positions = (jax.random.randint(ks[2], (BATCH, 1), 0, 1024, dtype=jnp.int32)
                 + jnp.arange(SEQ, dtype=jnp.int32)[None, :])
    return {
        'x': x,
        'mem': mem,
        'positions': positions,
        'attn_norm': gain(ks[3], (L, D_MODEL)),
        'w_in': nrm(ks[4], (L, D_MODEL, D_IN), D_MODEL ** -0.5),
        'moba_qk_gain': gain(ks[5], (L, 2, HEAD_DIM)),
        'diff_qk_gain': gain(ks[6], (L, 2, DIFF_QK_DIM)),
        'diff_lambda': nrm(ks[7], (L, 4, DIFF_QK_DIM), 0.1),
        'diff_subln': gain(ks[8], (L, DIFF_V_DIM)),
        'dsa_qk_gain': gain(ks[9], (L, 2, HEAD_DIM)),
        'w_out': nrm(ks[10], (L, D_MIX, D_MODEL), D_MIX ** -0.5),
        'cross_norm': gain(ks[11], (L, D_MODEL)),
        'mem_norm': gain(ks[12], (L, D_MODEL)),
        'cross_wq': nrm(ks[13], (L, D_MODEL, CROSS_W), D_MODEL ** -0.5),
        'cross_wkv': nrm(ks[14], (L, D_MODEL, 2 * CROSS_W), D_MODEL ** -0.5),
        'cross_qk_gain': gain(ks[15], (L, 2, HEAD_DIM)),
        'cross_wo': nrm(ks[16], (L, CROSS_W, D_MODEL), CROSS_W ** -0.5),
        'ffn_norm': gain(ks[17], (L, D_MODEL)),
        'ffn_w_up': nrm(ks[18], (L, D_MODEL, 2 * D_FF), D_MODEL ** -0.5),
        'ffn_conv_w': nrm(ks[19], (L, CONV_WIDTH, 2 * D_FF), CONV_WIDTH ** -0.5),
        'ffn_conv_b': nrm(ks[20], (L, 2 * D_FF), 0.02),
        'ffn_w_down': nrm(ks[21], (L, D_FF, D_MODEL), D_FF ** -0.5),
    }


def reference(x, mem, positions, attn_norm, w_in, moba_qk_gain, diff_qk_gain, diff_lambda,
              diff_subln, dsa_qk_gain, w_out, cross_norm, mem_norm, cross_wq, cross_wkv,
              cross_qk_gain, cross_wo, ffn_norm, ffn_w_up, ffn_conv_w, ffn_conv_b, ffn_w_down):
    b, s, _ = x.shape
    for l in range(DEPTH):
        h = rms_norm(x, attn_norm[l])
        (mq, mk, mv, dq, dk, dv, sq, sk, sv, iq, ik, iw) = _split_columns(h @ w_in[l])

        mq = rope(rms_norm(mq.reshape(b, s, MOBA_HEADS, HEAD_DIM), moba_qk_gain[l, 0]), positions)
        mk = rope(rms_norm(mk.reshape(b, s, MOBA_HEADS, HEAD_DIM), moba_qk_gain[l, 1]), positions)
        mv = mv.reshape(b, s, MOBA_HEADS, HEAD_DIM)
        o_moba = moba_attention(mq, mk, mv).reshape(b, s, MOBA_W)

        dq = rope(rms_norm(dq.reshape(b, s, 2 * DIFF_HEADS, DIFF_QK_DIM), diff_qk_gain[l, 0]),
                  positions).reshape(b, s, DIFF_HEADS, 2, DIFF_QK_DIM)
        dk = rope(rms_norm(dk.reshape(b, s, 2 * DIFF_HEADS, DIFF_QK_DIM), diff_qk_gain[l, 1]),
                  positions).reshape(b, s, DIFF_HEADS, 2, DIFF_QK_DIM)
        dv = dv.reshape(b, s, DIFF_HEADS, DIFF_V_DIM)
        lam_init = 0.8 - 0.6 * math.exp(-0.3 * l)
        lf = diff_lambda[l].astype(F32)
        lam = jnp.exp(jnp.sum(lf[0] * lf[1])) - jnp.exp(jnp.sum(lf[2] * lf[3])) + lam_init
        o_diff = diff_attention(dq, dk, dv, lam)
        o_diff = (rms_norm(o_diff, diff_subln[l]) * (1.0 - lam_init)).reshape(b, s, DIFF_V_W)

        sq = rope(rms_norm(sq.reshape(b, s, DSA_HEADS, HEAD_DIM), dsa_qk_gain[l, 0]), positions)
        sk = rope(rms_norm(sk.reshape(b, s, DSA_HEADS, HEAD_DIM), dsa_qk_gain[l, 1]), positions)
        sv = sv.reshape(b, s, DSA_HEADS, HEAD_DIM)
        iq = rope(iq.reshape(b, s, IDX_HEADS, IDX_DIM), positions)
        ik = rope(ik.reshape(b, s, 1, IDX_DIM), positions)[:, :, 0]
        iw = iw * (IDX_HEADS ** -0.5)
        o_dsa = dsa_attention(sq, sk, sv, iq, ik, iw).reshape(b, s, DSA_W)

        mixed = jnp.concatenate([o_moba, o_diff, o_dsa], axis=-1).astype(x.dtype)
        x = x + mixed @ w_out[l]

        h = rms_norm(x, cross_norm[l])
        m = rms_norm(mem, mem_norm[l])
        n_mem = mem.shape[1]
        cq = rms_norm((h @ cross_wq[l]).reshape(b, s, CROSS_HEADS, HEAD_DIM), cross_qk_gain[l, 0])
        ck, cv = jnp.split(m @ cross_wkv[l], 2, axis=-1)
        ck = rms_norm(ck.reshape(b, n_mem, CROSS_HEADS, HEAD_DIM), cross_qk_gain[l, 1])
        cv = cv.reshape(b, n_mem, CROSS_HEADS, HEAD_DIM)
        sc = jnp.einsum('bqhd,bmhd->bhqm', cq.astype(F32), ck.astype(F32)) * (HEAD_DIM ** -0.5)
        p = jax.nn.softmax(sc, axis=-1)
        co = jnp.einsum('bhqm,bmhd->bqhd', p, cv.astype(F32)).reshape(b, s, CROSS_W)
        x = x + co.astype(x.dtype) @ cross_wo[l]

        h = rms_norm(x, ffn_norm[l])
        u = h @ ffn_w_up[l]
        up = jnp.pad(u, ((0, 0), (CONV_WIDTH - 1, 0), (0, 0)))
        cw = ffn_conv_w[l]
        uc = (cw[0] * up[:, :-2] + cw[1] * up[:, 1:-1] + cw[2] * up[:, 2:] + ffn_conv_b[l])
        g, val = jnp.split(uc, 2, axis=-1)
        x = x + (jax.nn.silu(g) * val) @ ffn_w_down[l]
    return x
```

```python
import functools
import math

import jax
import jax.numpy as jnp
from jax import lax
from jax.experimental import pallas as pl
from jax.experimental.pallas import tpu as pltpu

F32 = jnp.float32
BF16 = jnp.bfloat16
I32 = jnp.int32

D_MODEL = 2048
DEPTH = 2
HEAD_DIM = 128
MOBA_HEADS = 6
DIFF_HEADS = 4
DSA_HEADS = 6
MOBA_BLOCK = 256
MOBA_TOPK = 3
DIFF_QK_DIM = 64
DSA_TOPK = 256
IDX_HEADS = 8
IDX_DIM = 64
MEM_LEN = 256
CROSS_HEADS = 4
CROSS_W = CROSS_HEADS * HEAD_DIM
D_FF = 5504
ROPE_THETA = 10000.0
RMS_EPS = 1e-6

LANES = 128
PROJ_TN = 256
D_IN = 6728
D_IN_PAD = 6912
N_PROJ_TILES = D_IN_PAD // PROJ_TN
N_HEAD_BLOCKS = D_IN_PAD // LANES
FF_TILE = 512
D_FF_PAD = 5632
NEG = -1e30
INT_MIN = -(2 ** 31)
VMEM_LIMIT = 56 * 1024 * 1024

BLK_MQ, BLK_MK, BLK_MV = 0, 6, 12
BLK_DQ, BLK_DK, BLK_DV = 18, 22, 26
BLK_SQ, BLK_SK, BLK_SV = 30, 36, 42
BLK_IQ, BLK_IK = 48, 52


def _cparams(sem):
    return pltpu.CompilerParams(dimension_semantics=sem, vmem_limit_bytes=VMEM_LIMIT)


def _rms_rows(x, g):
    return x * lax.rsqrt(jnp.mean(x * x, axis=-1, keepdims=True) + RMS_EPS) * g


def _dot_t(a, b):
    return lax.dot_general(a, b, (((1,), (1,)), ((), ())), preferred_element_type=F32)


def _rope_tab_kernel(pos_ref, inv128_ref, inv64_ref, c128_ref, s128_ref, c64_ref, s64_ref):
    p = pos_ref[...].astype(F32)
    lane = lax.broadcasted_iota(I32, (1, LANES), 1)
    a = p * inv128_ref[...]
    sa = jnp.sin(a)
    c128_ref[...] = jnp.cos(a)
    s128_ref[...] = jnp.where(lane < 64, -sa, sa)
    b = p * inv64_ref[...]
    sb = jnp.sin(b)
    c64_ref[...] = jnp.cos(b)
    s64_ref[...] = jnp.where((lane % 64) < 32, -sb, sb)


def _rope_tables(positions, s):
    tm = min(s, 1024)
    inv_a = ROPE_THETA ** (-(jnp.arange(64, dtype=F32) * 2.0 / 128))
    inv_b = ROPE_THETA ** (-(jnp.arange(32, dtype=F32) * 2.0 / 64))
    inv128 = jnp.tile(inv_a, 2).reshape(1, LANES)
    inv64 = jnp.tile(inv_b, 4).reshape(1, LANES)
    pos = positions.reshape(s, 1)
    tab = jax.ShapeDtypeStruct((s, LANES), F32)
    row = pl.BlockSpec((tm, LANES), lambda i: (i, 0))
    cst = pl.BlockSpec((1, LANES), lambda i: (0, 0))
    return pl.pallas_call(
        _rope_tab_kernel,
        grid=(s // tm,),
        in_specs=[pl.BlockSpec((tm, 1), lambda i: (i, 0)), cst, cst],
        out_specs=[row, row, row, row],
        out_shape=[tab, tab, tab, tab],
        compiler_params=_cparams(("arbitrary",)),
        name="rope_tables",
    )(pos, inv128, inv64)


def _proj_kernel(x_ref, g_ref, w_ref, c128_ref, s128_ref, c64_ref, s64_ref, gains_ref,
                 main_ref, tail_ref, h_scr):
    j = pl.program_id(1)

    @pl.when(j == 0)
    def _():
        h_scr[...] = _rms_rows(x_ref[...], g_ref[...]).astype(BF16)

    r = jnp.dot(h_scr[...], w_ref[...], preferred_element_type=F32)
    lane = lax.broadcasted_iota(I32, (1, LANES), 1)
    lo = lane < 64

    def rope128(y):
        return y * c128_ref[...] + pltpu.roll(y, 64, 1) * s128_ref[...]

    def rope64(y):
        partner = jnp.where((lane % 64) < 32, pltpu.roll(y, 96, 1), pltpu.roll(y, 32, 1))
        return y * c64_ref[...] + partner * s64_ref[...]

    def norm64(y, g):
        y2 = y * y
        s_lo = jnp.sum(jnp.where(lo, y2, 0.0), axis=-1, keepdims=True)
        s_hi = jnp.sum(jnp.where(lo, 0.0, y2), axis=-1, keepdims=True)
        ms = jnp.where(lo, s_lo, s_hi) * (1.0 / 64)
        return y * lax.rsqrt(ms + RMS_EPS) * g

    def emit(fn):
        for c in range(2):
            main_ref[c] = fn(r[:, c * LANES:(c + 1) * LANES]).astype(BF16)

    def qk128(row):
        return lambda y: rope128(_rms_rows(y, gains_ref[row:row + 1, :]))

    def qk64(row):
        return lambda y: rope64(norm64(y, gains_ref[row:row + 1, :]))

    ident = lambda y: y
    segments = (
        (0, 3, qk128(0)), (3, 6, qk128(1)), (6, 9, ident),
        (9, 11, qk64(4)), (11, 13, qk64(5)), (13, 15, ident),
        (15, 18, qk128(2)), (18, 21, qk128(3)), (21, 24, ident),
        (24, 26, rope64),
    )
    for first, last, fn in segments:
        pl.when((j >= first) & (j < last))(functools.partial(emit, fn))

    @pl.when(j == N_PROJ_TILES - 1)
    def _():
        y = r[:, :LANES]
        kk = jnp.where(lo, rope64(y), 0.0)
        main_ref[0] = kk.astype(BF16)
        main_ref[1] = pltpu.roll(kk, 64, 1).astype(BF16)
        tail_ref[...] = (y * (IDX_HEADS ** -0.5)) * (IDX_DIM ** -0.5)


def _in_proj(x, g, w_in_p, tabs, gains, s):
    tm = min(s, 512)
    c128, s128, c64, s64 = tabs
    row = pl.BlockSpec((tm, LANES), lambda i, j: (i, 0))
    return pl.pallas_call(
        _proj_kernel,
        grid=(s // tm, N_PROJ_TILES),
        in_specs=[
            pl.BlockSpec((tm, D_MODEL), lambda i, j: (i, 0)),
            pl.BlockSpec((1, D_MODEL), lambda i, j: (0, 0)),
            pl.BlockSpec((D_MODEL, PROJ_TN), lambda i, j: (0, j)),
            row, row, row, row,
            pl.BlockSpec((8, LANES), lambda i, j: (0, 0)),
        ],
        out_specs=[
            pl.BlockSpec((2, tm, LANES), lambda i, j: (j, i, 0)),
            pl.BlockSpec((tm, LANES), lambda i, j: (i, 0)),
        ],
        out_shape=[
            jax.ShapeDtypeStruct((N_HEAD_BLOCKS, s, LANES), BF16),
            jax.ShapeDtypeStruct((s, LANES), F32),
        ],
        scratch_shapes=[pltpu.VMEM((tm, D_MODEL), BF16)],
        compiler_params=_cparams(("arbitrary", "arbitrary")),
        name="in_proj",
    )(x, g, w_in_p, c128, s128, c64, s64, gains)


def _flash_step(q, kj, vj, mask, scale, carry):
    m, l, acc = carry
    sc = _dot_t(q, kj) * scale
    sc = jnp.where(mask, sc, NEG)
    m_new = jnp.maximum(m, jnp.max(sc, axis=-1, keepdims=True))
    p = jnp.where(mask, jnp.exp(sc - m_new), 0.0)
    alpha = jnp.exp(m - m_new)
    l = alpha * l + jnp.sum(p, axis=-1, keepdims=True)
    acc = alpha * acc + jnp.dot(p.astype(BF16), vj, preferred_element_type=F32)
    return m_new, l, acc


def _flash_init(rows):
    return (jnp.full((rows, 1), NEG, F32), jnp.zeros((rows, 1), F32),
            jnp.zeros((rows, HEAD_DIM), F32))


def _moba_kernel(q_ref, k_ref, v_ref, o_ref, kbar_scr, *, nb, n_sel):
    blk = MOBA_BLOCK
    i = pl.program_id(1)

    @pl.when(i == 0)
    def _():
        def mean_block(n, c):
            kk = k_ref[0, pl.ds(pl.multiple_of(n * blk, blk), blk), :].astype(F32)
            kbar_scr[pl.ds(n, 1), :] = jnp.sum(kk, axis=0, keepdims=True) * (1.0 / blk)
            return c
        lax.fori_loop(0, nb, mean_block, 0)

    q = q_ref[0]
    gate = _dot_t(q, kbar_scr[...].astype(BF16))
    bl = lax.broadcasted_iota(I32, (blk, nb), 1)
    valid = bl < i
    g = jnp.where(valid, gate, -jnp.inf)
    sel = jnp.zeros((blk, nb), F32)
    for _ in range(n_sel):
        mx = jnp.max(g, axis=-1, keepdims=True)
        first = jnp.min(jnp.where(g == mx, bl, nb), axis=-1, keepdims=True)
        hit = bl == first
        sel = jnp.where(hit, 1.0, sel)
        g = jnp.where(hit, -jnp.inf, g)
    sel = jnp.where(valid, sel, 0.0)

    scale = HEAD_DIM ** -0.5
    rows = lax.broadcasted_iota(I32, (blk, blk), 0)
    cols = lax.broadcasted_iota(I32, (blk, blk), 1)

    def kv(j):
        sl = pl.ds(pl.multiple_of(j * blk, blk), blk)
        return k_ref[0, sl, :], v_ref[0, sl, :]

    k_own, v_own = kv(i)
    carry = _flash_step(q, k_own, v_own, cols <= rows, scale, _flash_init(blk))

    def body(j, carry):
        picked = jnp.max(jnp.where(bl == j, sel, 0.0), axis=-1, keepdims=True) > 0.0
        kj, vj = kv(j)
        return _flash_step(q, kj, vj, jnp.broadcast_to(picked, (blk, blk)), scale, carry)

    _, l, acc = lax.fori_loop(0, i, body, carry)
    o_ref[0] = (acc / l).astype(BF16)


def _moba(main, s):
    blk = MOBA_BLOCK
    nb = s // blk
    n_sel = max(1, min(MOBA_TOPK, nb - 1))
    return pl.pallas_call(
        functools.partial(_moba_kernel, nb=nb, n_sel=n_sel),
        grid=(MOBA_HEADS, nb),
        in_specs=[
            pl.BlockSpec((1, blk, LANES), lambda h, i: (BLK_MQ + h, i, 0)),
            pl.BlockSpec((1, s, LANES), lambda h, i: (BLK_MK + h, 0, 0)),
            pl.BlockSpec((1, s, LANES), lambda h, i: (BLK_MV + h, 0, 0)),
        ],
        out_specs=pl.BlockSpec((1, blk, LANES), lambda h, i: (h, i, 0)),
        out_shape=jax.ShapeDtypeStruct((MOBA_HEADS, s, LANES), BF16),
        scratch_shapes=[pltpu.VMEM((nb, LANES), F32)],
        compiler_params=_cparams(("arbitrary", "arbitrary")),
        name="moba",
    )(main, main, main)


def _diff_kernel(q_ref, k_ref, v_ref, lam_ref, g_ref, o_ref, *, tq, lam_init):
    i = pl.program_id(1)
    q = q_ref[0]
    lane = lax.broadcasted_iota(I32, (tq, LANES), 1)
    zero = jnp.zeros_like(q)
    qq = jnp.concatenate([jnp.where(lane < 64, q, zero), jnp.where(lane < 64, zero, q)], axis=0)
    scale = DIFF_QK_DIM ** -0.5
    rows = lax.broadcasted_iota(I32, (2 * tq, tq), 0) & (tq - 1)
    cols = lax.broadcasted_iota(I32, (2 * tq, tq), 1)
    everything = jnp.full((2 * tq, tq), True)

    def kv(j):
        sl = pl.ds(pl.multiple_of(j * tq, tq), tq)
        return k_ref[0, sl, :], v_ref[0, sl, :]

    k_own, v_own = kv(i)
    carry = _flash_step(qq, k_own, v_own, cols <= rows, scale, _flash_init(2 * tq))

    def body(j, carry):
        kj, vj = kv(j)
        return _flash_step(qq, kj, vj, everything, scale, carry)

    _, l, acc = lax.fori_loop(0, i, body, carry)
    o = acc / l
    lf = lam_ref[...]
    lam = (jnp.exp(jnp.sum(lf[0:1] * lf[1:2], axis=-1, keepdims=True))
           - jnp.exp(jnp.sum(lf[2:3] * lf[3:4], axis=-1, keepdims=True)) + lam_init)
    a = o[:tq] - lam * o[tq:]
    o_ref[0] = (_rms_rows(a, g_ref[...]) * (1.0 - lam_init)).astype(BF16)


def _diff(main, lam_params, subln, layer, s):
    tq = min(s, 256)
    lam_init = 0.8 - 0.6 * math.exp(-0.3 * layer)
    return pl.pallas_call(
        functools.partial(_diff_kernel, tq=tq, lam_init=lam_init),
        grid=(DIFF_HEADS, s // tq),
        in_specs=[
            pl.BlockSpec((1, tq, LANES), lambda h, i: (BLK_DQ + h, i, 0)),
            pl.BlockSpec((1, s, LANES), lambda h, i: (BLK_DK + h, 0, 0)),
            pl.BlockSpec((1, s, LANES), lambda h, i: (BLK_DV + h, 0, 0)),
            pl.BlockSpec((4, DIFF_QK_DIM), lambda h, i: (0, 0)),
            pl.BlockSpec((1, LANES), lambda h, i: (0, 0)),
        ],
        out_specs=pl.BlockSpec((1, tq, LANES), lambda h, i: (h, i, 0)),
        out_shape=jax.ShapeDtypeStruct((DIFF_HEADS, s, LANES), BF16),
        compiler_params=_cparams(("arbitrary", "arbitrary")),
        name="diff_attn",
    )(main, main, main, lam_params, subln)


DSA_TQ = 256
DSA_SEARCH_W = 1024
DSA_ROW_CHUNK = 64


def _sort_key(sc):
    sc = jnp.where(sc == 0.0, 0.0, sc)
    b = lax.bitcast_convert_type(sc, I32)
    return b ^ ((b >> 31) & 0x7FFFFFFF)


def _dsa_kernel(qi_ref, ki_ref, w_ref, q_ref, k_ref, v_ref, o_ref,
                key_scr, wb_scr, thr_scr, *, n_keep):
    t = DSA_TQ
    i = pl.program_id(0)
    w = w_ref[...]
    for h in range(IDX_HEADS):
        wb_scr[h] = jnp.broadcast_to(w[:, 64 + h:65 + h], (t, t))

    def tile(j):
        return pl.ds(pl.multiple_of(j * t, t), t)

    def index_scores(j):
        k_lo = ki_ref[0, tile(j), :]
        k_hi = ki_ref[1, tile(j), :]
        sc = jnp.zeros((t, t), F32)
        for b in range(IDX_HEADS // 2):
            qb = qi_ref[b]
            sc = sc + wb_scr[2 * b] * jnp.maximum(_dot_t(qb, k_lo), 0.0)
            sc = sc + wb_scr[2 * b + 1] * jnp.maximum(_dot_t(qb, k_hi), 0.0)
        return sc

    rows = lax.broadcasted_iota(I32, (t, t), 0)
    cols = lax.broadcasted_iota(I32, (t, t), 1)
    causal = cols <= rows

    def fill_past(j, c):
        key_scr[:, tile(j)] = _sort_key(index_scores(j))
        return c
    lax.fori_loop(0, i, fill_past, 0)
    key_scr[:, tile(i)] = _sort_key(jnp.where(causal, index_scores(i), -jnp.inf))
    tiles_per_step = DSA_SEARCH_W // t
    n_steps = (i + tiles_per_step) // tiles_per_step

    def fill_pad(j, c):
        key_scr[:, tile(j)] = jnp.full((t, t), INT_MIN, I32)
        return c
    lax.fori_loop(i + 1, n_steps * tiles_per_step, fill_pad, 0)

    rc = DSA_ROW_CHUNK

    def search_rows(c, carry):
        r0 = pl.multiple_of(c * rc, rc)

        def bit_body(b, prefix):
            cand_u = prefix | lax.shift_left(jnp.int32(1), 31 - b)
            cand = cand_u ^ INT_MIN

            def count(st, cnt):
                kk = key_scr[pl.ds(r0, rc), pl.ds(pl.multiple_of(st * DSA_SEARCH_W, DSA_SEARCH_W),
                                                  DSA_SEARCH_W)]
                ge = jnp.where(kk >= cand, 1, 0)
                part = ge[:, 0:LANES]
                for u in range(1, DSA_SEARCH_W // LANES):
                    part = part + ge[:, u * LANES:(u + 1) * LANES]
                return cnt + part
            cnt = lax.fori_loop(0, n_steps, count, jnp.zeros((rc, LANES), I32))
            total = jnp.sum(cnt, axis=-1, keepdims=True)
            return jnp.where(total >= n_keep, cand_u, prefix)

        prefix = lax.fori_loop(0, 32, bit_body, jnp.zeros((rc, 1), I32))
        thr_scr[pl.ds(r0, rc), :] = jnp.broadcast_to(prefix ^ INT_MIN, (rc, t))
        return carry
    lax.fori_loop(0, t // rc, search_rows, 0)

    scale = HEAD_DIM ** -0.5
    thr = thr_scr[...]

    def head(h, c):
        q = q_ref[h]

        def step(j, extra, carry):
            mask = key_scr[:, tile(j)] >= thr
            if extra is not None:
                mask = mask & extra
            return _flash_step(q, k_ref[h, tile(j), :], v_ref[h, tile(j), :], mask, scale, carry)

        carry = lax.fori_loop(0, i, lambda j, cr: step(j, None, cr), _flash_init(t))
        _, l, acc = step(i, causal, carry)
        o_ref[h] = (acc / l).astype(BF16)
        return c
    lax.fori_loop(0, DSA_HEADS, head, 0)


def _dsa(main, tail, s):
    t = DSA_TQ
    n_keep = min(DSA_TOPK, s // 4)
    s_pad = -(-s // DSA_SEARCH_W) * DSA_SEARCH_W
    resident = dict(pipeline_mode=pl.Buffered(1))
    return pl.pallas_call(
        functools.partial(_dsa_kernel, n_keep=n_keep),
        grid=(s // t,),
        in_specs=[
            pl.BlockSpec((IDX_HEADS // 2, t, LANES), lambda i: (BLK_IQ // 4, i, 0)),
            pl.BlockSpec((2, s, LANES), lambda i: (BLK_IK // 2, 0, 0), **resident),
            pl.BlockSpec((t, LANES), lambda i: (i, 0)),
            pl.BlockSpec((DSA_HEADS, t, LANES), lambda i: (BLK_SQ // 6, i, 0)),
            pl.BlockSpec((DSA_HEADS, s, LANES), lambda i: (BLK_SK // 6, 0, 0), **resident),
            pl.BlockSpec((DSA_HEADS, s, LANES), lambda i: (BLK_SV // 6, 0, 0), **resident),
        ],
        out_specs=pl.BlockSpec((DSA_HEADS, t, LANES), lambda i: (0, i, 0)),
        out_shape=jax.ShapeDtypeStruct((DSA_HEADS, s, LANES), BF16),
        scratch_shapes=[
            pltpu.VMEM((t, s_pad), I32),
            pltpu.VMEM((IDX_HEADS, t, t), F32),
            pltpu.VMEM((t, t), I32),
        ],
        compiler_params=_cparams(("arbitrary",)),
        name="dsa",
    )(main, main, tail, main, main, main)


def _memkv_kernel(mem_ref, g_ref, wkv_ref, gk_ref, ck_ref, cv_ref):
    m = _rms_rows(mem_ref[...], g_ref[...]).astype(BF16)
    kv = jnp.dot(m, wkv_ref[...], preferred_element_type=F32)
    for h in range(CROSS_HEADS):
        sl = slice(h * HEAD_DIM, (h + 1) * HEAD_DIM)
        ck_ref[:, sl] = _rms_rows(kv[:, sl], gk_ref[...]).astype(BF16)
    cv_ref[...] = kv[:, CROSS_W:].astype(BF16)


def _memkv(mem2d, g, wkv, gk):
    n = mem2d.shape[0]
    out = jax.ShapeDtypeStruct((n, CROSS_W), BF16)
    return pl.pallas_call(
        _memkv_kernel,
        out_shape=[out, out],
        compiler_params=pltpu.CompilerParams(vmem_limit_bytes=VMEM_LIMIT),
        name="mem_kv",
    )(mem2d, g, wkv, gk)


def _mid_kernel(x_ref, om_ref, od_ref, os_ref, wout_ref, gc_ref, wq_ref, gq_ref,
                ck_ref, cv_ref, wo_ref, o_ref):
    heads = ([om_ref[h] for h in range(MOBA_HEADS)] + [od_ref[h] for h in range(DIFF_HEADS)]
             + [os_ref[h] for h in range(DSA_HEADS)])
    mixed = jnp.concatenate(heads, axis=-1)
    x1 = x_ref[...] + jnp.dot(mixed, wout_ref[...], preferred_element_type=F32)
    hq = _rms_rows(x1, gc_ref[...]).astype(BF16)
    cq = jnp.dot(hq, wq_ref[...], preferred_element_type=F32)
    scale = HEAD_DIM ** -0.5
    outs = []
    for h in range(CROSS_HEADS):
        sl = slice(h * HEAD_DIM, (h + 1) * HEAD_DIM)
        qh = _rms_rows(cq[:, sl], gq_ref[...]).astype(BF16)
        sc = _dot_t(qh, ck_ref[:, sl]) * scale
        m = jnp.max(sc, axis=-1, keepdims=True)
        p = jnp.exp(sc - m)
        l = jnp.sum(p, axis=-1, keepdims=True)
        outs.append(jnp.dot(p.astype(BF16), cv_ref[:, sl], preferred_element_type=F32) / l)
    co = jnp.concatenate(outs, axis=-1).astype(BF16)
    o_ref[...] = x1 + jnp.dot(co, wo_ref[...], preferred_element_type=F32)


def _mid(x, om, od, osa, wout, gc, wq, gq, ck, cv, wo, s):
    tm = min(s, 512)
    n_mem = ck.shape[0]
    whole = lambda shape: pl.BlockSpec(shape, lambda i: (0,) * len(shape), pipeline_mode=pl.Buffered(1))
    heads = lambda n: pl.BlockSpec((n, tm, LANES), lambda i: (0, i, 0))
    return pl.pallas_call(
        _mid_kernel,
        grid=(s // tm,),
        in_specs=[
            pl.BlockSpec((tm, D_MODEL), lambda i: (i, 0)),
            heads(MOBA_HEADS), heads(DIFF_HEADS), heads(DSA_HEADS),
            whole((D_MODEL, D_MODEL)), whole((1, D_MODEL)), whole((D_MODEL, CROSS_W)),
            whole((1, LANES)), whole((n_mem, CROSS_W)), whole((n_mem, CROSS_W)),
            whole((CROSS_W, D_MODEL)),
        ],
        out_specs=pl.BlockSpec((tm, D_MODEL), lambda i: (i, 0)),
        out_shape=jax.ShapeDtypeStruct((s, D_MODEL), F32),
        compiler_params=_cparams(("arbitrary",)),
        name="out_proj_cross",
    )(x, om, od, osa, wout, gc, wq, gq, ck, cv, wo)


FFN_HALO = 16


def _ffn_kernel(x_ref, xp_ref, g_ref, wg_ref, wv_ref, cwg_ref, cwv_ref, cbg_ref, cbv_ref,
                wd_ref, o_ref, h_scr, *, tm):
    i = pl.program_id(0)
    f = pl.program_id(1)

    @pl.when(f == 0)
    def _():
        prev = _rms_rows(xp_ref[...], g_ref[...])
        h_scr[0:FFN_HALO, :] = jnp.where(i > 0, prev, 0.0).astype(BF16)
        h_scr[FFN_HALO:, :] = _rms_rows(x_ref[...], g_ref[...]).astype(BF16)

    h = h_scr[...]

    def conv(w_ref, cw_ref, cb_ref):
        u = jnp.dot(h, w_ref[...], preferred_element_type=F32)
        cw = cw_ref[...]
        uc = (cw[0:1] * pltpu.roll(u, 2, 0) + cw[1:2] * pltpu.roll(u, 1, 0) + cw[2:3] * u
              + cb_ref[...])
        return uc[FFN_HALO:, :]

    gate = conv(wg_ref, cwg_ref, cbg_ref)
    val = conv(wv_ref, cwv_ref, cbv_ref)
    act = (gate * jax.nn.sigmoid(gate) * val).astype(BF16)
    y = jnp.dot(act, wd_ref[...], preferred_element_type=F32)

    @pl.when(f == 0)
    def _():
        o_ref[...] = x_ref[...] + y

    @pl.when(f > 0)
    def _():
        o_ref[...] += y


def _ffn(x, g, w_up_p, cw_p, cb_p, w_down_p, s):
    tm = min(s, 512)
    nf = D_FF_PAD // FF_TILE
    halo_blocks = tm // FFN_HALO
    return pl.pallas_call(
        functools.partial(_ffn_kernel, tm=tm),
        grid=(s // tm, nf),
        in_specs=[
            pl.BlockSpec((tm, D_MODEL), lambda i, f: (i, 0)),
            pl.BlockSpec((FFN_HALO, D_MODEL), lambda i, f: (jnp.maximum(i * halo_blocks - 1, 0), 0)),
            pl.BlockSpec((1, D_MODEL), lambda i, f: (0, 0)),
            pl.BlockSpec((D_MODEL, FF_TILE), lambda i, f: (0, f)),
            pl.BlockSpec((D_MODEL, FF_TILE), lambda i, f: (0, f + nf)),
            pl.BlockSpec((3, FF_TILE), lambda i, f: (0, f)),
            pl.BlockSpec((3, FF_TILE), lambda i, f: (0, f + nf)),
            pl.BlockSpec((1, FF_TILE), lambda i, f: (0, f)),
            pl.BlockSpec((1, FF_TILE), lambda i, f: (0, f + nf)),
            pl.BlockSpec((FF_TILE, D_MODEL), lambda i, f: (f, 0)),
        ],
        out_specs=pl.BlockSpec((tm, D_MODEL), lambda i, f: (i, 0)),
        out_shape=jax.ShapeDtypeStruct((s, D_MODEL), F32),
        scratch_shapes=[pltpu.VMEM((tm + FFN_HALO, D_MODEL), BF16)],
        compiler_params=_cparams(("arbitrary", "arbitrary")),
        name="ffn",
    )(x, x, g, w_up_p, w_up_p, cw_p, cw_p, cb_p, cb_p, w_down_p)


def _pad_cols(a, width):
    return jnp.pad(a, ((0, 0), (0, width - a.shape[1])))


def _split_pad(a):
    return jnp.concatenate([_pad_cols(a[:, :D_FF], D_FF_PAD), _pad_cols(a[:, D_FF:], D_FF_PAD)], axis=1)


def kernel(x, mem, positions, attn_norm, w_in, moba_qk_gain, diff_qk_gain, diff_lambda,
           diff_subln, dsa_qk_gain, w_out, cross_norm, mem_norm, cross_wq, cross_wkv,
           cross_qk_gain, cross_wo, ffn_norm, ffn_w_up, ffn_conv_w, ffn_conv_b, ffn_w_down):
    b, s, _ = x.shape
    assert b == 1 and s % DSA_TQ == 0 and s % MOBA_BLOCK == 0
    xs = x.reshape(s, D_MODEL)
    mem2d = mem.reshape(mem.shape[1], D_MODEL)
    tabs = _rope_tables(positions, s)
    row = lambda v: v.reshape(1, -1)
    for l in range(DEPTH):
        gains = jnp.concatenate([
            moba_qk_gain[l], dsa_qk_gain[l], jnp.tile(diff_qk_gain[l], (1, 2)),
            jnp.zeros((2, LANES), F32)], axis=0)
        w_in_p = _pad_cols(w_in[l], D_IN_PAD).astype(BF16)
        main, tail = _in_proj(xs, row(attn_norm[l]), w_in_p, tabs, gains, s)
        o_moba = _moba(main, s)
        o_diff = _diff(main, diff_lambda[l], row(diff_subln[l]), l, s)
        o_dsa = _dsa(main, tail, s)
        ck, cv = _memkv(mem2d, row(mem_norm[l]), cross_wkv[l].astype(BF16), row(cross_qk_gain[l, 1]))
        xs = _mid(xs, o_moba, o_diff, o_dsa, w_out[l].astype(BF16), row(cross_norm[l]),
                  cross_wq[l].astype(BF16), row(cross_qk_gain[l, 0]), ck, cv,
                  cross_wo[l].astype(BF16), s)
        w_up_p = _split_pad(ffn_w_up[l]).astype(BF16)
        cw_p = _split_pad(ffn_conv_w[l])
        cb_p = _split_pad(row(ffn_conv_b[l]))
        w_down_p = jnp.pad(ffn_w_down[l], ((0, D_FF_PAD - D_FF), (0, 0))).astype(BF16)
        xs = _ffn(xs, row(ffn_norm[l]), w_up_p, cw_p, cb_p, w_down_p, s)
    return xs.reshape(b, s, D_MODEL)
```

```python
import functools
import math

import jax
import jax.numpy as jnp
from jax import lax
from jax.experimental import pallas as pl
from jax.experimental.pallas import tpu as pltpu

F32 = jnp.float32
BF16 = jnp.bfloat16
I32 = jnp.int32

D_MODEL = 2048
DEPTH = 2
HEAD_DIM = 128
MOBA_HEADS = 6
DIFF_HEADS = 4
DSA_HEADS = 6
MOBA_BLOCK = 256
MOBA_TOPK = 3
DIFF_QK_DIM = 64
DSA_TOPK = 256
IDX_HEADS = 8
IDX_DIM = 64
MEM_LEN = 256
CROSS_HEADS = 4
CROSS_W = CROSS_HEADS * HEAD_DIM
D_FF = 5504
ROPE_THETA = 10000.0
RMS_EPS = 1e-6

LANES = 128
PROJ_TN = 256
D_IN = 6728
D_IN_PAD = 6912
N_PROJ_TILES = D_IN_PAD // PROJ_TN
N_HEAD_BLOCKS = D_IN_PAD // LANES
FF_TILE = 512
D_FF_PAD = 5632
NEG = -1e30
LOG2E = 1.4426950408889634
FLASH_WIDTH = 4
FLASH_ROW_SPLIT = 1
INT_MIN = -(2 ** 31)
VMEM_LIMIT = 56 * 1024 * 1024

BLK_MQ, BLK_MK, BLK_MV = 0, 6, 12
BLK_DQ, BLK_DK, BLK_DV = 18, 22, 26
BLK_SQ, BLK_SK, BLK_SV = 30, 36, 42
BLK_IQ, BLK_IK = 48, 52


def _cparams(sem):
    return pltpu.CompilerParams(dimension_semantics=sem, vmem_limit_bytes=VMEM_LIMIT)


def _rms_rows(x, g):
    return x * lax.rsqrt(jnp.mean(x * x, axis=-1, keepdims=True) + RMS_EPS) * g


def _dot_t(a, b):
    return lax.dot_general(a, b, (((1,), (1,)), ((), ())), preferred_element_type=F32)


def _rope_tab_kernel(pos_ref, inv128_ref, inv64_ref, c128_ref, s128_ref, c64_ref, s64_ref):
    p = pos_ref[...].astype(F32)
    lane = lax.broadcasted_iota(I32, (1, LANES), 1)
    a = p * inv128_ref[...]
    sa = jnp.sin(a)
    c128_ref[...] = jnp.cos(a)
    s128_ref[...] = jnp.where(lane < 64, -sa, sa)
    b = p * inv64_ref[...]
    sb = jnp.sin(b)
    c64_ref[...] = jnp.cos(b)
    s64_ref[...] = jnp.where((lane % 64) < 32, -sb, sb)


def _rope_tables(positions, s):
    tm = min(s, 1024)
    inv_a = ROPE_THETA ** (-(jnp.arange(64, dtype=F32) * 2.0 / 128))
    inv_b = ROPE_THETA ** (-(jnp.arange(32, dtype=F32) * 2.0 / 64))
    inv128 = jnp.tile(inv_a, 2).reshape(1, LANES)
    inv64 = jnp.tile(inv_b, 4).reshape(1, LANES)
    pos = positions.reshape(s, 1)
    tab = jax.ShapeDtypeStruct((s, LANES), F32)
    row = pl.BlockSpec((tm, LANES), lambda i: (i, 0))
    cst = pl.BlockSpec((1, LANES), lambda i: (0, 0))
    return pl.pallas_call(
        _rope_tab_kernel,
        grid=(s // tm,),
        in_specs=[pl.BlockSpec((tm, 1), lambda i: (i, 0)), cst, cst],
        out_specs=[row, row, row, row],
        out_shape=[tab, tab, tab, tab],
        compiler_params=_cparams(("arbitrary",)),
        name="rope_tables",
    )(pos, inv128, inv64)


def _proj_kernel(x_ref, g_ref, w_ref, c128_ref, s128_ref, c64_ref, s64_ref, gains_ref,
                 main_ref, tail_ref, h_scr):
    j = pl.program_id(1)

    @pl.when(j == 0)
    def _():
        h_scr[...] = _rms_rows(x_ref[...], g_ref[...]).astype(BF16)

    r = jnp.dot(h_scr[...], w_ref[...], preferred_element_type=F32)
    lane = lax.broadcasted_iota(I32, (1, LANES), 1)
    lo = lane < 64

    def rope128(y):
        return y * c128_ref[...] + pltpu.roll(y, 64, 1) * s128_ref[...]

    def rope64(y):
        partner = jnp.where((lane % 64) < 32, pltpu.roll(y, 96, 1), pltpu.roll(y, 32, 1))
        return y * c64_ref[...] + partner * s64_ref[...]

    def norm64(y, g):
        y2 = y * y
        s_lo = jnp.sum(jnp.where(lo, y2, 0.0), axis=-1, keepdims=True)
        s_hi = jnp.sum(jnp.where(lo, 0.0, y2), axis=-1, keepdims=True)
        ms = jnp.where(lo, s_lo, s_hi) * (1.0 / 64)
        return y * lax.rsqrt(ms + RMS_EPS) * g

    def emit(fn):
        for c in range(2):
            main_ref[c] = fn(r[:, c * LANES:(c + 1) * LANES]).astype(BF16)

    def qk128(row):
        return lambda y: rope128(_rms_rows(y, gains_ref[row:row + 1, :]))

    def qk64(row):
        return lambda y: rope64(norm64(y, gains_ref[row:row + 1, :]))

    ident = lambda y: y
    segments = (
        (0, 3, qk128(0)), (3, 6, qk128(1)), (6, 9, ident),
        (9, 11, qk64(4)), (11, 13, qk64(5)), (13, 15, ident),
        (15, 18, qk128(2)), (18, 21, qk128(3)), (21, 24, ident),
        (24, 26, rope64),
    )
    for first, last, fn in segments:
        pl.when((j >= first) & (j < last))(functools.partial(emit, fn))

    @pl.when(j == N_PROJ_TILES - 1)
    def _():
        y = r[:, :LANES]
        kk = jnp.where(lo, rope64(y), 0.0)
        main_ref[0] = kk.astype(BF16)
        main_ref[1] = pltpu.roll(kk, 64, 1).astype(BF16)
        tail_ref[...] = (y * (IDX_HEADS ** -0.5)) * (IDX_DIM ** -0.5)


def _in_proj(x, g, w_in_p, tabs, gains, s):
    tm = min(s, 512)
    c128, s128, c64, s64 = tabs
    row = pl.BlockSpec((tm, LANES), lambda i, j: (i, 0))
    return pl.pallas_call(
        _proj_kernel,
        grid=(s // tm, N_PROJ_TILES),
        in_specs=[
            pl.BlockSpec((tm, D_MODEL), lambda i, j: (i, 0)),
            pl.BlockSpec((1, D_MODEL), lambda i, j: (0, 0)),
            pl.BlockSpec((D_MODEL, PROJ_TN), lambda i, j: (0, j)),
            row, row, row, row,
            pl.BlockSpec((8, LANES), lambda i, j: (0, 0)),
        ],
        out_specs=[
            pl.BlockSpec((2, tm, LANES), lambda i, j: (j, i, 0)),
            pl.BlockSpec((tm, LANES), lambda i, j: (i, 0)),
        ],
        out_shape=[
            jax.ShapeDtypeStruct((N_HEAD_BLOCKS, s, LANES), BF16),
            jax.ShapeDtypeStruct((s, LANES), F32),
        ],
        scratch_shapes=[pltpu.VMEM((tm, D_MODEL), BF16)],
        compiler_params=_cparams(("arbitrary", "arbitrary")),
        name="in_proj",
    )(x, g, w_in_p, c128, s128, c64, s64, gains)


def _row_blocks(rows):
    rb = rows // FLASH_ROW_SPLIT
    return [(r * rb, (r + 1) * rb) for r in range(FLASH_ROW_SPLIT)]


def _flash_slab(q, tiles, scale, states):
    out = []
    for (r0, r1), st in zip(_row_blocks(q.shape[0]), states):
        sub = [(k, v, None if fn is None else fn(r0, r1)) for k, v, fn in tiles]
        out.append(_flash_rows(q[r0:r1], sub, scale, st))
    return tuple(out)


def _flash_rows(q, tiles, scale, carry):
    m, l, acc = carry
    c = scale * LOG2E
    scs = []
    for kj, _, mask in tiles:
        sc = _dot_t(q, kj)
        scs.append(sc if mask is None else jnp.where(mask, sc, NEG))
    top = scs[0]
    for sc in scs[1:]:
        top = jnp.maximum(top, sc)
    m_new = jnp.maximum(m, jnp.max(top, axis=-1, keepdims=True))
    tot = None
    pv = None
    for sc, (_, vj, _) in zip(scs, tiles):
        p = jnp.exp2((sc - m_new) * c)
        tot = p if tot is None else tot + p
        d = jnp.dot(p.astype(BF16), vj, preferred_element_type=F32)
        pv = d if pv is None else pv + d
    alpha = jnp.exp2((m - m_new) * c)
    l = alpha * l + jnp.sum(tot, axis=-1, keepdims=True)
    acc = alpha * acc + pv
    return m_new, l, acc


def _flash_init(rows):
    rb = rows // FLASH_ROW_SPLIT
    return tuple((jnp.full((rb, 1), NEG, F32), jnp.zeros((rb, 1), F32),
                  jnp.zeros((rb, HEAD_DIM), F32)) for _ in range(FLASH_ROW_SPLIT))


def _flash_loop(q, first_slab, n_slabs, load_tile, scale, states):
    def slab(g, st):
        return _flash_slab(q, [load_tile(g * FLASH_WIDTH + u) for u in range(FLASH_WIDTH)], scale, st)
    return lax.fori_loop(first_slab, n_slabs, slab, states)


def _flash_finish(states):
    return jnp.concatenate([acc / l for _, l, acc in states], axis=0)


def _moba_kernel(q_ref, k_ref, v_ref, o_ref, kbar_scr, *, nb, n_sel):
    blk = MOBA_BLOCK
    i = pl.program_id(1)

    @pl.when(i == 0)
    def _():
        def mean_block(n, c):
            kk = k_ref[0, pl.ds(pl.multiple_of(n * blk, blk), blk), :].astype(F32)
            kbar_scr[pl.ds(n, 1), :] = jnp.sum(kk, axis=0, keepdims=True) * (1.0 / blk)
            return c
        lax.fori_loop(0, nb, mean_block, 0)

    q = q_ref[0]
    gate = _dot_t(q, kbar_scr[...].astype(BF16))
    bl = lax.broadcasted_iota(I32, (blk, nb), 1)
    valid = bl < i
    g = jnp.where(valid, gate, -jnp.inf)
    sel = jnp.zeros((blk, nb), F32)
    for _ in range(n_sel):
        mx = jnp.max(g, axis=-1, keepdims=True)
        first = jnp.min(jnp.where(g == mx, bl, nb), axis=-1, keepdims=True)
        hit = bl == first
        sel = jnp.where(hit, 1.0, sel)
        g = jnp.where(hit, -jnp.inf, g)
    sel = jnp.where(valid, sel, 0.0)

    scale = HEAD_DIM ** -0.5

    def load(j):
        def admitted(r0, r1):
            n = r1 - r0
            picked = jnp.max(jnp.where(bl[r0:r1] == j, sel[r0:r1], 0.0), axis=-1, keepdims=True) > 0.0
            row = lax.broadcasted_iota(I32, (n, 1), 0) + r0
            lim = jnp.where(j == i, row, jnp.where(picked, blk - 1, -1))
            return lax.broadcasted_iota(I32, (n, blk), 1) <= lim
        sl = pl.ds(pl.multiple_of(jnp.minimum(j, nb - 1) * blk, blk), blk)
        return k_ref[0, sl, :], v_ref[0, sl, :], admitted

    n_slabs = (i + FLASH_WIDTH) // FLASH_WIDTH
    o_ref[0] = _flash_finish(_flash_loop(q, 0, n_slabs, load, scale, _flash_init(blk))).astype(BF16)


def _moba(main, s):
    blk = MOBA_BLOCK
    nb = s // blk
    n_sel = max(1, min(MOBA_TOPK, nb - 1))
    return pl.pallas_call(
        functools.partial(_moba_kernel, nb=nb, n_sel=n_sel),
        grid=(MOBA_HEADS, nb),
        in_specs=[
            pl.BlockSpec((1, blk, LANES), lambda h, i: (BLK_MQ + h, i, 0)),
            pl.BlockSpec((1, s, LANES), lambda h, i: (BLK_MK + h, 0, 0)),
            pl.BlockSpec((1, s, LANES), lambda h, i: (BLK_MV + h, 0, 0)),
        ],
        out_specs=pl.BlockSpec((1, blk, LANES), lambda h, i: (h, i, 0)),
        out_shape=jax.ShapeDtypeStruct((MOBA_HEADS, s, LANES), BF16),
        scratch_shapes=[pltpu.VMEM((nb, LANES), F32)],
        compiler_params=_cparams(("arbitrary", "arbitrary")),
        name="moba",
    )(main, main, main)


def _diff_kernel(q_ref, k_ref, v_ref, lam_ref, g_ref, o_ref, *, tq, nq, lam_init):
    i = pl.program_id(1)
    q = q_ref[0]
    lane = lax.broadcasted_iota(I32, (tq, LANES), 1)
    zero = jnp.zeros_like(q)
    qq = jnp.concatenate([jnp.where(lane < 64, q, zero), jnp.where(lane < 64, zero, q)], axis=0)
    scale = DIFF_QK_DIM ** -0.5

    def kv(j):
        sl = pl.ds(pl.multiple_of(jnp.minimum(j, nq - 1) * tq, tq), tq)
        return k_ref[0, sl, :], v_ref[0, sl, :]

    def load_tail(j):
        def admitted(r0, r1):
            n = r1 - r0
            row = (lax.broadcasted_iota(I32, (n, 1), 0) + r0) & (tq - 1)
            lim = jnp.where(j < i, tq - 1, jnp.where(j == i, row, -1))
            return lax.broadcasted_iota(I32, (n, tq), 1) <= lim
        return kv(j) + (admitted,)

    full = i // FLASH_WIDTH
    st = _flash_loop(qq, 0, full, lambda j: kv(j) + (None,), scale, _flash_init(2 * tq))
    o = _flash_finish(_flash_loop(qq, full, full + 1, load_tail, scale, st))
    lf = lam_ref[...]
    lam = (jnp.exp(jnp.sum(lf[0:1] * lf[1:2], axis=-1, keepdims=True))
           - jnp.exp(jnp.sum(lf[2:3] * lf[3:4], axis=-1, keepdims=True)) + lam_init)
    a = o[:tq] - lam * o[tq:]
    o_ref[0] = (_rms_rows(a, g_ref[...]) * (1.0 - lam_init)).astype(BF16)


def _diff(main, lam_params, subln, layer, s):
    tq = min(s, 256)
    lam_init = 0.8 - 0.6 * math.exp(-0.3 * layer)
    return pl.pallas_call(
        functools.partial(_diff_kernel, tq=tq, nq=s // tq, lam_init=lam_init),
        grid=(DIFF_HEADS, s // tq),
        in_specs=[
            pl.BlockSpec((1, tq, LANES), lambda h, i: (BLK_DQ + h, i, 0)),
            pl.BlockSpec((1, s, LANES), lambda h, i: (BLK_DK + h, 0, 0)),
            pl.BlockSpec((1, s, LANES), lambda h, i: (BLK_DV + h, 0, 0)),
            pl.BlockSpec((4, DIFF_QK_DIM), lambda h, i: (0, 0)),
            pl.BlockSpec((1, LANES), lambda h, i: (0, 0)),
        ],
        out_specs=pl.BlockSpec((1, tq, LANES), lambda h, i: (h, i, 0)),
        out_shape=jax.ShapeDtypeStruct((DIFF_HEADS, s, LANES), BF16),
        compiler_params=_cparams(("arbitrary", "arbitrary")),
        name="diff_attn",
    )(main, main, main, lam_params, subln)


DSA_TQ = 256
DSA_SEARCH_W = FLASH_WIDTH * DSA_TQ
DSA_ROW_CHUNK = 256


def _sort_key(sc):
    sc = jnp.where(sc == 0.0, 0.0, sc)
    b = lax.bitcast_convert_type(sc, I32)
    return b ^ ((b >> 31) & 0x7FFFFFFF)


def _dsa_kernel(qi_ref, ki_ref, w_ref, q_ref, k_ref, v_ref, o_ref,
                key_scr, wb_scr, thr_scr, *, n_keep):
    t = DSA_TQ
    i = pl.program_id(0)
    w = w_ref[...]
    for h in range(IDX_HEADS):
        wb_scr[h] = jnp.broadcast_to(w[:, 64 + h:65 + h], (t, t))

    def tile(j):
        return pl.ds(pl.multiple_of(j * t, t), t)

    def index_scores(j):
        k_lo = ki_ref[0, tile(j), :]
        k_hi = ki_ref[1, tile(j), :]
        sc = jnp.zeros((t, t), F32)
        for b in range(IDX_HEADS // 2):
            qb = qi_ref[b]
            sc = sc + wb_scr[2 * b] * jnp.maximum(_dot_t(qb, k_lo), 0.0)
            sc = sc + wb_scr[2 * b + 1] * jnp.maximum(_dot_t(qb, k_hi), 0.0)
        return sc

    rows = lax.broadcasted_iota(I32, (t, t), 0)
    cols = lax.broadcasted_iota(I32, (t, t), 1)
    causal = cols <= rows

    def fill_past(j, c):
        key_scr[:, tile(j)] = _sort_key(index_scores(j))
        return c
    lax.fori_loop(0, i, fill_past, 0)
    key_scr[:, tile(i)] = jnp.where(causal, _sort_key(index_scores(i)), INT_MIN)
    tiles_per_step = DSA_SEARCH_W // t
    n_steps = (i + tiles_per_step) // tiles_per_step

    def fill_pad(j, c):
        key_scr[:, tile(j)] = jnp.full((t, t), INT_MIN, I32)
        return c
    lax.fori_loop(i + 1, n_steps * tiles_per_step, fill_pad, 0)

    rc = DSA_ROW_CHUNK

    def search_rows(c, carry):
        r0 = pl.multiple_of(c * rc, rc)

        def bit_body(b, prefix):
            cand_u = prefix | lax.shift_left(jnp.int32(1), 31 - b)
            cand = cand_u ^ INT_MIN

            def count(st, cnt):
                kk = key_scr[pl.ds(r0, rc), pl.ds(pl.multiple_of(st * DSA_SEARCH_W, DSA_SEARCH_W),
                                                  DSA_SEARCH_W)]
                ge = jnp.where(kk >= cand, 1, 0)
                part = ge[:, 0:LANES]
                for u in range(1, DSA_SEARCH_W // LANES):
                    part = part + ge[:, u * LANES:(u + 1) * LANES]
                return cnt + part
            cnt = lax.fori_loop(0, n_steps, count, jnp.zeros((rc, LANES), I32))
            total = jnp.sum(cnt, axis=-1, keepdims=True)
            return jnp.where(total >= n_keep, cand_u, prefix)

        prefix = lax.fori_loop(0, 32, bit_body, jnp.zeros((rc, 1), I32))
        thr = jnp.maximum(prefix ^ INT_MIN, INT_MIN + 1)
        thr_scr[pl.ds(r0, rc), :] = jnp.broadcast_to(thr, (rc, t))
        return carry
    lax.fori_loop(0, t // rc, search_rows, 0)

    scale = HEAD_DIM ** -0.5

    def head(h, c):
        q = q_ref[h]

        def load(j):
            return (k_ref[h, tile(j), :], v_ref[h, tile(j), :],
                    lambda r0, r1: key_scr[r0:r1, tile(j)] >= thr_scr[r0:r1, :])

        o_ref[h] = _flash_finish(_flash_loop(q, 0, n_steps, load, scale, _flash_init(t))).astype(BF16)
        return c
    lax.fori_loop(0, DSA_HEADS, head, 0)


def _dsa(main, tail, s):
    t = DSA_TQ
    n_keep = min(DSA_TOPK, s // 4)
    s_pad = -(-s // DSA_SEARCH_W) * DSA_SEARCH_W
    resident = dict(pipeline_mode=pl.Buffered(1))
    return pl.pallas_call(
        functools.partial(_dsa_kernel, n_keep=n_keep),
        grid=(s // t,),
        in_specs=[
            pl.BlockSpec((IDX_HEADS // 2, t, LANES), lambda i: (BLK_IQ // 4, i, 0)),
            pl.BlockSpec((2, s, LANES), lambda i: (BLK_IK // 2, 0, 0), **resident),
            pl.BlockSpec((t, LANES), lambda i: (i, 0)),
            pl.BlockSpec((DSA_HEADS, t, LANES), lambda i: (BLK_SQ // 6, i, 0)),
            pl.BlockSpec((DSA_HEADS, s, LANES), lambda i: (BLK_SK // 6, 0, 0), **resident),
            pl.BlockSpec((DSA_HEADS, s, LANES), lambda i: (BLK_SV // 6, 0, 0), **resident),
        ],
        out_specs=pl.BlockSpec((DSA_HEADS, t, LANES), lambda i: (0, i, 0)),
        out_shape=jax.ShapeDtypeStruct((DSA_HEADS, s, LANES), BF16),
        scratch_shapes=[
            pltpu.VMEM((t, s_pad), I32),
            pltpu.VMEM((IDX_HEADS, t, t), F32),
            pltpu.VMEM((t, t), I32),
        ],
        compiler_params=_cparams(("arbitrary",)),
        name="dsa",
    )(main, main, tail, main, main, main)


def _memkv_kernel(mem_ref, g_ref, wkv_ref, gk_ref, ck_ref, cv_ref):
    m = _rms_rows(mem_ref[...], g_ref[...]).astype(BF16)
    kv = jnp.dot(m, wkv_ref[...], preferred_element_type=F32)
    for h in range(CROSS_HEADS):
        sl = slice(h * HEAD_DIM, (h + 1) * HEAD_DIM)
        ck_ref[:, sl] = _rms_rows(kv[:, sl], gk_ref[...]).astype(BF16)
    cv_ref[...] = kv[:, CROSS_W:].astype(BF16)


def _memkv(mem2d, g, wkv, gk):
    n = mem2d.shape[0]
    out = jax.ShapeDtypeStruct((n, CROSS_W), BF16)
    return pl.pallas_call(
        _memkv_kernel,
        out_shape=[out, out],
        compiler_params=pltpu.CompilerParams(vmem_limit_bytes=VMEM_LIMIT),
        name="mem_kv",
    )(mem2d, g, wkv, gk)


def _mid_kernel(x_ref, om_ref, od_ref, os_ref, wout_ref, gc_ref, wq_ref, gq_ref,
                ck_ref, cv_ref, wo_ref, o_ref):
    heads = ([om_ref[h] for h in range(MOBA_HEADS)] + [od_ref[h] for h in range(DIFF_HEADS)]
             + [os_ref[h] for h in range(DSA_HEADS)])
    mixed = jnp.concatenate(heads, axis=-1)
    x1 = x_ref[...] + jnp.dot(mixed, wout_ref[...], preferred_element_type=F32)
    hq = _rms_rows(x1, gc_ref[...]).astype(BF16)
    cq = jnp.dot(hq, wq_ref[...], preferred_element_type=F32)
    scale = HEAD_DIM ** -0.5
    outs = []
    for h in range(CROSS_HEADS):
        sl = slice(h * HEAD_DIM, (h + 1) * HEAD_DIM)
        qh = _rms_rows(cq[:, sl], gq_ref[...]).astype(BF16)
        sc = _dot_t(qh, ck_ref[:, sl]) * scale
        m = jnp.max(sc, axis=-1, keepdims=True)
        p = jnp.exp(sc - m)
        l = jnp.sum(p, axis=-1, keepdims=True)
        outs.append(jnp.dot(p.astype(BF16), cv_ref[:, sl], preferred_element_type=F32) / l)
    co = jnp.concatenate(outs, axis=-1).astype(BF16)
    o_ref[...] = x1 + jnp.dot(co, wo_ref[...], preferred_element_type=F32)


def _mid(x, om, od, osa, wout, gc, wq, gq, ck, cv, wo, s):
    tm = min(s, 512)
    n_mem = ck.shape[0]
    whole = lambda shape: pl.BlockSpec(shape, lambda i: (0,) * len(shape), pipeline_mode=pl.Buffered(1))
    heads = lambda n: pl.BlockSpec((n, tm, LANES), lambda i: (0, i, 0))
    return pl.pallas_call(
        _mid_kernel,
        grid=(s // tm,),
        in_specs=[
            pl.BlockSpec((tm, D_MODEL), lambda i: (i, 0)),
            heads(MOBA_HEADS), heads(DIFF_HEADS), heads(DSA_HEADS),
            whole((D_MODEL, D_MODEL)), whole((1, D_MODEL)), whole((D_MODEL, CROSS_W)),
            whole((1, LANES)), whole((n_mem, CROSS_W)), whole((n_mem, CROSS_W)),
            whole((CROSS_W, D_MODEL)),
        ],
        out_specs=pl.BlockSpec((tm, D_MODEL), lambda i: (i, 0)),
        out_shape=jax.ShapeDtypeStruct((s, D_MODEL), F32),
        compiler_params=_cparams(("arbitrary",)),
        name="out_proj_cross",
    )(x, om, od, osa, wout, gc, wq, gq, ck, cv, wo)


FFN_HALO = 16


def _ffn_kernel(x_ref, xp_ref, g_ref, wg_ref, wv_ref, cwg_ref, cwv_ref, cbg_ref, cbv_ref,
                wd_ref, o_ref, h_scr, *, tm):
    i = pl.program_id(0)
    f = pl.program_id(1)

    @pl.when(f == 0)
    def _():
        prev = _rms_rows(xp_ref[...], g_ref[...])
        h_scr[0:FFN_HALO, :] = jnp.where(i > 0, prev, 0.0).astype(BF16)
        h_scr[FFN_HALO:, :] = _rms_rows(x_ref[...], g_ref[...]).astype(BF16)

    h = h_scr[...]

    def conv(w_ref, cw_ref, cb_ref):
        u = jnp.dot(h, w_ref[...], preferred_element_type=F32)
        cw = cw_ref[...]
        uc = (cw[0:1] * pltpu.roll(u, 2, 0) + cw[1:2] * pltpu.roll(u, 1, 0) + cw[2:3] * u
              + cb_ref[...])
        return uc[FFN_HALO:, :]

    gate = conv(wg_ref, cwg_ref, cbg_ref)
    val = conv(wv_ref, cwv_ref, cbv_ref)
    act = (gate * jax.nn.sigmoid(gate) * val).astype(BF16)
    y = jnp.dot(act, wd_ref[...], preferred_element_type=F32)

    @pl.when(f == 0)
    def _():
        o_ref[...] = x_ref[...] + y

    @pl.when(f > 0)
    def _():
        o_ref[...] += y


def _ffn(x, g, w_up_p, cw_p, cb_p, w_down_p, s):
    tm = min(s, 512)
    nf = D_FF_PAD // FF_TILE
    halo_blocks = tm // FFN_HALO
    return pl.pallas_call(
        functools.partial(_ffn_kernel, tm=tm),
        grid=(s // tm, nf),
        in_specs=[
            pl.BlockSpec((tm, D_MODEL), lambda i, f: (i, 0)),
            pl.BlockSpec((FFN_HALO, D_MODEL), lambda i, f: (jnp.maximum(i * halo_blocks - 1, 0), 0)),
            pl.BlockSpec((1, D_MODEL), lambda i, f: (0, 0)),
            pl.BlockSpec((D_MODEL, FF_TILE), lambda i, f: (0, f)),
            pl.BlockSpec((D_MODEL, FF_TILE), lambda i, f: (0, f + nf)),
            pl.BlockSpec((3, FF_TILE), lambda i, f: (0, f)),
            pl.BlockSpec((3, FF_TILE), lambda i, f: (0, f + nf)),
            pl.BlockSpec((1, FF_TILE), lambda i, f: (0, f)),
            pl.BlockSpec((1, FF_TILE), lambda i, f: (0, f + nf)),
            pl.BlockSpec((FF_TILE, D_MODEL), lambda i, f: (f, 0)),
        ],
        out_specs=pl.BlockSpec((tm, D_MODEL), lambda i, f: (i, 0)),
        out_shape=jax.ShapeDtypeStruct((s, D_MODEL), F32),
        scratch_shapes=[pltpu.VMEM((tm + FFN_HALO, D_MODEL), BF16)],
        compiler_params=_cparams(("arbitrary", "arbitrary")),
        name="ffn",
    )(x, x, g, w_up_p, w_up_p, cw_p, cw_p, cb_p, cb_p, w_down_p)


def _pad_cols(a, width):
    return jnp.pad(a, ((0, 0), (0, width - a.shape[1])))


def _split_pad(a):
    return jnp.concatenate([_pad_cols(a[:, :D_FF], D_FF_PAD), _pad_cols(a[:, D_FF:], D_FF_PAD)], axis=1)


def kernel(x, mem, positions, attn_norm, w_in, moba_qk_gain, diff_qk_gain, diff_lambda,
           diff_subln, dsa_qk_gain, w_out, cross_norm, mem_norm, cross_wq, cross_wkv,
           cross_qk_gain, cross_wo, ffn_norm, ffn_w_up, ffn_conv_w, ffn_conv_b, ffn_w_down):
    b, s, _ = x.shape
    assert b == 1 and s % DSA_TQ == 0 and s % MOBA_BLOCK == 0
    xs = x.reshape(s, D_MODEL)
    mem2d = mem.reshape(mem.shape[1], D_MODEL)
    tabs = _rope_tables(positions, s)
    row = lambda v: v.reshape(1, -1)
    for l in range(DEPTH):
        gains = jnp.concatenate([
            moba_qk_gain[l], dsa_qk_gain[l], jnp.tile(diff_qk_gain[l], (1, 2)),
            jnp.zeros((2, LANES), F32)], axis=0)
        w_in_p = _pad_cols(w_in[l], D_IN_PAD).astype(BF16)
        main, tail = _in_proj(xs, row(attn_norm[l]), w_in_p, tabs, gains, s)
        o_moba = _moba(main, s)
        o_diff = _diff(main, diff_lambda[l], row(diff_subln[l]), l, s)
        o_dsa = _dsa(main, tail, s)
        ck, cv = _memkv(mem2d, row(mem_norm[l]), cross_wkv[l].astype(BF16), row(cross_qk_gain[l, 1]))
        xs = _mid(xs, o_moba, o_diff, o_dsa, w_out[l].astype(BF16), row(cross_norm[l]),
                  cross_wq[l].astype(BF16), row(cross_qk_gain[l, 0]), ck, cv,
                  cross_wo[l].astype(BF16), s)
        w_up_p = _split_pad(ffn_w_up[l]).astype(BF16)
        cw_p = _split_pad(ffn_conv_w[l])
        cb_p = _split_pad(row(ffn_conv_b[l]))
        w_down_p = jnp.pad(ffn_w_down[l], ((0, D_FF_PAD - D_FF), (0, 0))).astype(BF16)
        xs = _ffn(xs, row(ffn_norm[l]), w_up_p, cw_p, cb_p, w_down_p, s)
    return xs.reshape(b, s, D_MODEL)
```

```python
import functools
import math

import jax
import jax.numpy as jnp
from jax import lax
from jax.experimental import pallas as pl
from jax.experimental.pallas import tpu as pltpu

F32 = jnp.float32
BF16 = jnp.bfloat16
I32 = jnp.int32

D_MODEL = 2048
DEPTH = 2
HEAD_DIM = 128
MOBA_HEADS = 6
DIFF_HEADS = 4
DSA_HEADS = 6
MOBA_BLOCK = 256
MOBA_TOPK = 3
DIFF_QK_DIM = 64
DSA_TOPK = 256
IDX_HEADS = 8
IDX_DIM = 64
MEM_LEN = 256
CROSS_HEADS = 4
CROSS_W = CROSS_HEADS * HEAD_DIM
D_FF = 5504
ROPE_THETA = 10000.0
RMS_EPS = 1e-6

LANES = 128
PROJ_TN = 256
D_IN = 6728
D_IN_PAD = 6912
N_PROJ_TILES = D_IN_PAD // PROJ_TN
N_HEAD_BLOCKS = D_IN_PAD // LANES
FF_TILE = 512
D_FF_PAD = 5632
NEG = -1e30
LOG2E = 1.4426950408889634
FLASH_WIDTH = 4
FLASH_ROW_SPLIT = 1
MOBA_HEADS_PER_STEP = 3
DSA_HEADS_PER_STEP = 3
INT_MIN = -(2 ** 31)
VMEM_LIMIT = 56 * 1024 * 1024

BLK_MQ, BLK_MK, BLK_MV = 0, 6, 12
BLK_DQ, BLK_DK, BLK_DV = 18, 22, 26
BLK_SQ, BLK_SK, BLK_SV = 30, 36, 42
BLK_IQ, BLK_IK = 48, 52


def _cparams(sem):
    return pltpu.CompilerParams(dimension_semantics=sem, vmem_limit_bytes=VMEM_LIMIT)


def _rms_rows(x, g):
    return x * lax.rsqrt(jnp.mean(x * x, axis=-1, keepdims=True) + RMS_EPS) * g


def _dot_t(a, b):
    return lax.dot_general(a, b, (((1,), (1,)), ((), ())), preferred_element_type=F32)


def _rope_tab_kernel(pos_ref, inv128_ref, inv64_ref, c128_ref, s128_ref, c64_ref, s64_ref):
    p = pos_ref[...].astype(F32)
    lane = lax.broadcasted_iota(I32, (1, LANES), 1)
    a = p * inv128_ref[...]
    sa = jnp.sin(a)
    c128_ref[...] = jnp.cos(a)
    s128_ref[...] = jnp.where(lane < 64, -sa, sa)
    b = p * inv64_ref[...]
    sb = jnp.sin(b)
    c64_ref[...] = jnp.cos(b)
    s64_ref[...] = jnp.where((lane % 64) < 32, -sb, sb)


def _rope_tables(positions, s):
    tm = min(s, 1024)
    inv_a = ROPE_THETA ** (-(jnp.arange(64, dtype=F32) * 2.0 / 128))
    inv_b = ROPE_THETA ** (-(jnp.arange(32, dtype=F32) * 2.0 / 64))
    inv128 = jnp.tile(inv_a, 2).reshape(1, LANES)
    inv64 = jnp.tile(inv_b, 4).reshape(1, LANES)
    pos = positions.reshape(s, 1)
    tab = jax.ShapeDtypeStruct((s, LANES), F32)
    row = pl.BlockSpec((tm, LANES), lambda i: (i, 0))
    cst = pl.BlockSpec((1, LANES), lambda i: (0, 0))
    return pl.pallas_call(
        _rope_tab_kernel,
        grid=(s // tm,),
        in_specs=[pl.BlockSpec((tm, 1), lambda i: (i, 0)), cst, cst],
        out_specs=[row, row, row, row],
        out_shape=[tab, tab, tab, tab],
        compiler_params=_cparams(("arbitrary",)),
        name="rope_tables",
    )(pos, inv128, inv64)


def _proj_kernel(x_ref, g_ref, w_ref, c128_ref, s128_ref, c64_ref, s64_ref, gains_ref,
                 main_ref, tail_ref, h_scr):
    j = pl.program_id(1)

    @pl.when(j == 0)
    def _():
        h_scr[...] = _rms_rows(x_ref[...], g_ref[...]).astype(BF16)

    r = jnp.dot(h_scr[...], w_ref[...], preferred_element_type=F32)
    lane = lax.broadcasted_iota(I32, (1, LANES), 1)
    lo = lane < 64

    def rope128(y):
        return y * c128_ref[...] + pltpu.roll(y, 64, 1) * s128_ref[...]

    def rope64(y):
        partner = jnp.where((lane % 64) < 32, pltpu.roll(y, 96, 1), pltpu.roll(y, 32, 1))
        return y * c64_ref[...] + partner * s64_ref[...]

    def norm64(y, g):
        y2 = y * y
        s_lo = jnp.sum(jnp.where(lo, y2, 0.0), axis=-1, keepdims=True)
        s_hi = jnp.sum(jnp.where(lo, 0.0, y2), axis=-1, keepdims=True)
        ms = jnp.where(lo, s_lo, s_hi) * (1.0 / 64)
        return y * lax.rsqrt(ms + RMS_EPS) * g

    def emit(fn):
        for c in range(2):
            main_ref[c] = fn(r[:, c * LANES:(c + 1) * LANES]).astype(BF16)

    def qk128(row, post=None):
        fn = lambda y: rope128(_rms_rows(y, gains_ref[row:row + 1, :]))
        return fn if post is None else (lambda y: fn(y) * post)

    def qk64(row, post=None):
        fn = lambda y: rope64(norm64(y, gains_ref[row:row + 1, :]))
        return fn if post is None else (lambda y: fn(y) * post)

    ident = lambda y: y
    q128 = HEAD_DIM ** -0.5 * LOG2E
    q64 = DIFF_QK_DIM ** -0.5 * LOG2E
    segments = (
        (0, 3, qk128(0, q128)), (3, 6, qk128(1)), (6, 9, ident),
        (9, 11, qk64(4, q64)), (11, 13, qk64(5)), (13, 15, ident),
        (15, 18, qk128(2, q128)), (18, 21, qk128(3)), (21, 24, ident),
        (24, 26, rope64),
    )
    for first, last, fn in segments:
        pl.when((j >= first) & (j < last))(functools.partial(emit, fn))

    @pl.when(j == N_PROJ_TILES - 1)
    def _():
        y = r[:, :LANES]
        kk = jnp.where(lo, rope64(y), 0.0)
        main_ref[0] = kk.astype(BF16)
        main_ref[1] = pltpu.roll(kk, 64, 1).astype(BF16)
        tail_ref[...] = (y * (IDX_HEADS ** -0.5)) * (IDX_DIM ** -0.5)


def _in_proj(x, g, w_in_p, tabs, gains, s):
    tm = min(s, 512)
    c128, s128, c64, s64 = tabs
    row = pl.BlockSpec((tm, LANES), lambda i, j: (i, 0))
    return pl.pallas_call(
        _proj_kernel,
        grid=(s // tm, N_PROJ_TILES),
        in_specs=[
            pl.BlockSpec((tm, D_MODEL), lambda i, j: (i, 0)),
            pl.BlockSpec((1, D_MODEL), lambda i, j: (0, 0)),
            pl.BlockSpec((D_MODEL, PROJ_TN), lambda i, j: (0, j)),
            row, row, row, row,
            pl.BlockSpec((8, LANES), lambda i, j: (0, 0)),
        ],
        out_specs=[
            pl.BlockSpec((2, tm, LANES), lambda i, j: (j, i, 0)),
            pl.BlockSpec((tm, LANES), lambda i, j: (i, 0)),
        ],
        out_shape=[
            jax.ShapeDtypeStruct((N_HEAD_BLOCKS, s, LANES), BF16),
            jax.ShapeDtypeStruct((s, LANES), F32),
        ],
        scratch_shapes=[pltpu.VMEM((tm, D_MODEL), BF16)],
        compiler_params=_cparams(("arbitrary", "arbitrary")),
        name="in_proj",
    )(x, g, w_in_p, c128, s128, c64, s64, gains)


def _flash_slab(q, k_slab, v_slab, masks, carry):
    m, l, acc = carry
    sc = _dot_t(q, k_slab)
    if masks is not None:
        tk = sc.shape[1] // len(masks)
        sc = jnp.concatenate([jnp.where(mk, sc[:, u * tk:(u + 1) * tk], NEG)
                              for u, mk in enumerate(masks)], axis=1)
    m_new = jnp.maximum(m, jnp.max(sc, axis=-1, keepdims=True))
    p = jnp.exp2(sc - m_new)
    alpha = jnp.exp2(m - m_new)
    l = alpha * l + jnp.sum(p, axis=-1, keepdims=True)
    acc = alpha * acc + jnp.dot(p.astype(BF16), v_slab, preferred_element_type=F32)
    return m_new, l, acc


def _flash_init(rows):
    return (jnp.full((rows, 1), NEG, F32), jnp.zeros((rows, 1), F32),
            jnp.zeros((rows, HEAD_DIM), F32))


def _flash_loop(streams, first_slab, n_slabs, states, shared_masks=None):
    def slab(g, sts):
        common = None if shared_masks is None else shared_masks(g)
        out = []
        for (q, load_slab), st in zip(streams, sts):
            k_slab, v_slab, masks = load_slab(g)
            out.append(_flash_slab(q, k_slab, v_slab, masks if common is None else common, st))
        return tuple(out)
    return lax.fori_loop(first_slab, n_slabs, slab, tuple(states))


def _flash_finish(state):
    _, l, acc = state
    return acc / l


def _moba_kernel(q_ref, k_ref, v_ref, o_ref, kbar_scr, *, nb, n_sel):
    blk = MOBA_BLOCK
    i = pl.program_id(1)

    @pl.when(i == 0)
    def _():
        kbar_scr[...] = jnp.zeros_like(kbar_scr)
        for hh in range(MOBA_HEADS_PER_STEP):
            def mean_block(n, c, hh=hh):
                kk = k_ref[hh, pl.ds(pl.multiple_of(n * blk, blk), blk), :].astype(F32)
                kbar_scr[hh, pl.ds(n, 1), :] = jnp.sum(kk, axis=0, keepdims=True) * (1.0 / blk)
                return c
            lax.fori_loop(0, nb, mean_block, 0)

    blk_id = lax.broadcasted_iota(I32, (LANES, blk), 0)
    valid = blk_id < i
    bl = lax.broadcasted_iota(I32, (blk, LANES), 1)

    def stream(hh):
        q = q_ref[hh]
        gate = _dot_t(kbar_scr[hh].astype(BF16), q)
        g = jnp.where(valid, gate, -jnp.inf)
        sel_t = jnp.zeros((LANES, blk), F32)
        for _ in range(n_sel):
            mx = jnp.max(g, axis=0, keepdims=True)
            first = jnp.min(jnp.where(g == mx, blk_id, LANES), axis=0, keepdims=True)
            hit = blk_id == first
            sel_t = jnp.where(hit, 1.0, sel_t)
            g = jnp.where(hit, -jnp.inf, g)
        sel = jnp.where(valid, sel_t, 0.0).T

        lim_tab = jnp.where(bl == i, row, jnp.where(sel > 0.0, blk - 1.0, -1.0))

        def load(g):
            masks = []
            for u in range(FLASH_WIDTH):
                j = g * FLASH_WIDTH + u
                lim = jnp.max(jnp.where(bl == j, lim_tab, -1.0), axis=-1, keepdims=True)
                masks.append(cols <= lim)
            sl = pl.ds(pl.multiple_of(g * slab_keys, slab_keys), slab_keys)
            return k_ref[hh, sl, :], v_ref[hh, sl, :], masks
        return q, load

    row = lax.broadcasted_iota(I32, (blk, LANES), 0).astype(F32)
    cols = lax.broadcasted_iota(I32, (blk, blk), 1).astype(F32)
    slab_keys = FLASH_WIDTH * blk
    streams = [stream(hh) for hh in range(MOBA_HEADS_PER_STEP)]
    n_slabs = (i + FLASH_WIDTH) // FLASH_WIDTH
    states = _flash_loop(streams, 0, n_slabs, [_flash_init(blk) for _ in streams])
    for hh, st in enumerate(states):
        o_ref[hh] = _flash_finish(st).astype(BF16)


def _moba(main, s):
    blk = MOBA_BLOCK
    nb = s // blk
    hps = MOBA_HEADS_PER_STEP
    assert nb <= LANES and nb % FLASH_WIDTH == 0 and MOBA_HEADS % hps == 0
    n_sel = max(1, min(MOBA_TOPK, nb - 1))
    return pl.pallas_call(
        functools.partial(_moba_kernel, nb=nb, n_sel=n_sel),
        grid=(MOBA_HEADS // hps, nb),
        in_specs=[
            pl.BlockSpec((hps, blk, LANES), lambda h, i: (BLK_MQ // hps + h, i, 0)),
            pl.BlockSpec((hps, s, LANES), lambda h, i: (BLK_MK // hps + h, 0, 0)),
            pl.BlockSpec((hps, s, LANES), lambda h, i: (BLK_MV // hps + h, 0, 0)),
        ],
        out_specs=pl.BlockSpec((hps, blk, LANES), lambda h, i: (h, i, 0)),
        out_shape=jax.ShapeDtypeStruct((MOBA_HEADS, s, LANES), BF16),
        scratch_shapes=[pltpu.VMEM((hps, LANES, LANES), F32)],
        compiler_params=_cparams(("arbitrary", "arbitrary")),
        name="moba",
    )(main, main, main)


def _diff_kernel(q_ref, k_ref, v_ref, lam_ref, g_ref, o_ref, *, tq, lam_init):
    i = pl.program_id(1)
    q = q_ref[0]
    lane = lax.broadcasted_iota(I32, (tq, LANES), 1)
    zero = jnp.zeros_like(q)
    qq = jnp.concatenate([jnp.where(lane < 64, q, zero), jnp.where(lane < 64, zero, q)], axis=0)
    slab_keys = FLASH_WIDTH * tq
    row = lax.broadcasted_iota(I32, (2 * tq, 1), 0) & (tq - 1)
    cols = lax.broadcasted_iota(I32, (2 * tq, tq), 1)

    def kv(g):
        sl = pl.ds(pl.multiple_of(g * slab_keys, slab_keys), slab_keys)
        return k_ref[0, sl, :], v_ref[0, sl, :]

    def load_tail(g):
        masks = []
        for u in range(FLASH_WIDTH):
            j = g * FLASH_WIDTH + u
            masks.append(cols <= jnp.where(j < i, tq - 1, jnp.where(j == i, row, -1)))
        return kv(g) + (masks,)

    full = i // FLASH_WIDTH
    st = _flash_loop([(qq, lambda g: kv(g) + (None,))], 0, full, [_flash_init(2 * tq)])
    st = _flash_loop([(qq, load_tail)], full, full + 1, st)
    o = _flash_finish(st[0])
    lf = lam_ref[...]
    lam = (jnp.exp(jnp.sum(lf[0:1] * lf[1:2], axis=-1, keepdims=True))
           - jnp.exp(jnp.sum(lf[2:3] * lf[3:4], axis=-1, keepdims=True)) + lam_init)
    a = o[:tq] - lam * o[tq:]
    o_ref[0] = (_rms_rows(a, g_ref[...]) * (1.0 - lam_init)).astype(BF16)


def _diff(main, lam_params, subln, layer, s):
    tq = min(s, 256)
    lam_init = 0.8 - 0.6 * math.exp(-0.3 * layer)
    return pl.pallas_call(
        functools.partial(_diff_kernel, tq=tq, lam_init=lam_init),
        grid=(DIFF_HEADS, s // tq),
        in_specs=[
            pl.BlockSpec((1, tq, LANES), lambda h, i: (BLK_DQ + h, i, 0)),
            pl.BlockSpec((1, s, LANES), lambda h, i: (BLK_DK + h, 0, 0)),
            pl.BlockSpec((1, s, LANES), lambda h, i: (BLK_DV + h, 0, 0)),
            pl.BlockSpec((4, DIFF_QK_DIM), lambda h, i: (0, 0)),
            pl.BlockSpec((1, LANES), lambda h, i: (0, 0)),
        ],
        out_specs=pl.BlockSpec((1, tq, LANES), lambda h, i: (h, i, 0)),
        out_shape=jax.ShapeDtypeStruct((DIFF_HEADS, s, LANES), BF16),
        compiler_params=_cparams(("arbitrary", "arbitrary")),
        name="diff_attn",
    )(main, main, main, lam_params, subln)


DSA_TQ = 256


def _sort_key(sc):
    sc = jnp.where(sc == 0.0, 0.0, sc)
    b = lax.bitcast_convert_type(sc, I32)
    return b ^ ((b >> 31) & 0x7FFFFFFF)


def _dsa_kernel(qi_ref, ki_ref, w_ref, q_ref, k_ref, v_ref, o_ref,
                key_scr, keyt_scr, wb_scr, thr_scr, *, n_keep, col_bits):
    t = DSA_TQ
    i = pl.program_id(0)
    w = w_ref[...]
    for h in range(IDX_HEADS):
        wb_scr[h] = jnp.broadcast_to(w[:, 64 + h:65 + h], (t, t))

    def tile(j):
        return pl.ds(pl.multiple_of(j * t, t), t)

    def index_scores(j):
        k_lo = ki_ref[0, tile(j), :]
        k_hi = ki_ref[1, tile(j), :]
        sc = jnp.zeros((t, t), F32)
        for b in range(IDX_HEADS // 2):
            qb = qi_ref[b]
            sc = sc + wb_scr[2 * b] * jnp.maximum(_dot_t(qb, k_lo), 0.0)
            sc = sc + wb_scr[2 * b + 1] * jnp.maximum(_dot_t(qb, k_hi), 0.0)
        return sc

    rows = lax.broadcasted_iota(I32, (t, t), 0)
    cols = lax.broadcasted_iota(I32, (t, t), 1)
    causal = cols <= rows

    def put_keys(j, keys):
        key_scr[:, tile(j)] = keys
        keyt_scr[tile(j), :] = keys.T

    def fill_past(j, c):
        put_keys(j, _sort_key(index_scores(j)))
        return c
    lax.fori_loop(0, i, fill_past, 0)
    put_keys(i, jnp.where(causal, _sort_key(index_scores(i)), INT_MIN))
    n_steps = (i + FLASH_WIDTH) // FLASH_WIDTH

    def fill_pad(j, c):
        key_scr[:, tile(j)] = jnp.full((t, t), INT_MIN, I32)
        return c
    lax.fori_loop(i + 1, n_steps * FLASH_WIDTH, fill_pad, 0)

    def count_keys(preds):
        def step(j, cnts):
            kk = keyt_scr[tile(j), :].reshape(t // 8, 8, t)
            return tuple(c + jnp.sum(jnp.where(p(kk), 1, 0), axis=0) for c, p in zip(cnts, preds))
        zero = jnp.zeros((8, t), I32)
        cnts = lax.fori_loop(0, i + 1, step, tuple(zero for _ in preds))
        return [jnp.sum(c, axis=0, keepdims=True) for c in cnts]

    def bit_body(b, prefix):
        cand_u = prefix | lax.shift_left(jnp.int32(1), 31 - b)
        cand = cand_u ^ INT_MIN
        total, = count_keys([lambda kk: kk >= cand])
        return jnp.where(total >= n_keep, cand_u, prefix)

    prefix = lax.fori_loop(0, 32, bit_body, jnp.zeros((8, t), I32))
    thr_t = jnp.maximum(prefix ^ INT_MIN, INT_MIN + 1)
    thr_scr[...] = jnp.broadcast_to(thr_t[0:1], (t, t)).T

    n_ge_t, = count_keys([lambda kk: kk >= thr_t])
    surplus = jnp.max(jnp.where(n_ge_t > n_keep, 1, 0))

    @pl.when(surplus > 0)
    def _():
        thr = thr_scr[:, 0:LANES]
        lane = lax.broadcasted_iota(I32, (t, LANES), 1)

        def chunks(st):
            base = st * (FLASH_WIDTH * t)
            return [pl.multiple_of(base + u * LANES, LANES) for u in range(FLASH_WIDTH * t // LANES)]

        def count_rows(pred):
            def step(st, cnt):
                for c0 in chunks(st):
                    cnt = cnt + jnp.where(pred(key_scr[:, pl.ds(c0, LANES)], c0), 1, 0)
                return cnt
            cnt = lax.fori_loop(0, n_steps, step, jnp.zeros((t, LANES), I32))
            return jnp.sum(cnt, axis=-1, keepdims=True)

        need = n_keep - count_rows(lambda kk, c0: kk > thr)

        def col_bit(b, last):
            cand = last | lax.shift_left(jnp.int32(1), col_bits - 1 - b)
            ties = count_rows(lambda kk, c0: jnp.where(kk == thr, lane + c0, cand) < cand)
            return jnp.where(ties < need, cand, last)
        last = lax.fori_loop(0, col_bits, col_bit, jnp.zeros((t, 1), I32))

        def drop(st, c):
            for c0 in chunks(st):
                kk = key_scr[:, pl.ds(c0, LANES)]
                late = jnp.where(kk == thr, lane + c0, last) > last
                key_scr[:, pl.ds(c0, LANES)] = jnp.where(late, INT_MIN, kk)
            return c
        lax.fori_loop(0, n_steps, drop, 0)

    slab_keys = FLASH_WIDTH * t

    def admitted(g):
        return [key_scr[:, tile(g * FLASH_WIDTH + u)] >= thr_scr[...] for u in range(FLASH_WIDTH)]

    def head_group(hg, c):
        def stream(h):
            def load(g):
                sl = pl.ds(pl.multiple_of(g * slab_keys, slab_keys), slab_keys)
                return k_ref[h, sl, :], v_ref[h, sl, :], None
            return q_ref[h], load

        heads = [hg * DSA_HEADS_PER_STEP + u for u in range(DSA_HEADS_PER_STEP)]
        states = _flash_loop([stream(h) for h in heads], 0, n_steps, [_flash_init(t) for _ in heads],
                             shared_masks=admitted)
        for h, st in zip(heads, states):
            o_ref[h] = _flash_finish(st).astype(BF16)
        return c
    lax.fori_loop(0, DSA_HEADS // DSA_HEADS_PER_STEP, head_group, 0)


def _dsa(main, tail, s):
    t = DSA_TQ
    n_keep = min(DSA_TOPK, s // 4)
    slab = FLASH_WIDTH * t
    s_pad = -(-s // slab) * slab
    resident = dict(pipeline_mode=pl.Buffered(1))
    return pl.pallas_call(
        functools.partial(_dsa_kernel, n_keep=n_keep, col_bits=(s_pad - 1).bit_length()),
        grid=(s // t,),
        in_specs=[
            pl.BlockSpec((IDX_HEADS // 2, t, LANES), lambda i: (BLK_IQ // 4, i, 0)),
            pl.BlockSpec((2, s, LANES), lambda i: (BLK_IK // 2, 0, 0), **resident),
            pl.BlockSpec((t, LANES), lambda i: (i, 0)),
            pl.BlockSpec((DSA_HEADS, t, LANES), lambda i: (BLK_SQ // 6, i, 0)),
            pl.BlockSpec((DSA_HEADS, s, LANES), lambda i: (BLK_SK // 6, 0, 0), **resident),
            pl.BlockSpec((DSA_HEADS, s, LANES), lambda i: (BLK_SV // 6, 0, 0), **resident),
        ],
        out_specs=pl.BlockSpec((DSA_HEADS, t, LANES), lambda i: (0, i, 0)),
        out_shape=jax.ShapeDtypeStruct((DSA_HEADS, s, LANES), BF16),
        scratch_shapes=[
            pltpu.VMEM((t, s_pad), I32),
            pltpu.VMEM((s, t), I32),
            pltpu.VMEM((IDX_HEADS, t, t), F32),
            pltpu.VMEM((t, t), I32),
        ],
        compiler_params=_cparams(("arbitrary",)),
        name="dsa",
    )(main, main, tail, main, main, main)


def _memkv_kernel(mem_ref, g_ref, wkv_ref, gk_ref, ck_ref, cv_ref):
    m = _rms_rows(mem_ref[...], g_ref[...]).astype(BF16)
    kv = jnp.dot(m, wkv_ref[...], preferred_element_type=F32)
    for h in range(CROSS_HEADS):
        sl = slice(h * HEAD_DIM, (h + 1) * HEAD_DIM)
        ck_ref[:, sl] = _rms_rows(kv[:, sl], gk_ref[...]).astype(BF16)
    cv_ref[...] = kv[:, CROSS_W:].astype(BF16)


def _memkv(mem2d, g, wkv, gk):
    n = mem2d.shape[0]
    out = jax.ShapeDtypeStruct((n, CROSS_W), BF16)
    return pl.pallas_call(
        _memkv_kernel,
        out_shape=[out, out],
        compiler_params=pltpu.CompilerParams(vmem_limit_bytes=VMEM_LIMIT),
        name="mem_kv",
    )(mem2d, g, wkv, gk)


def _mid_kernel(x_ref, om_ref, od_ref, os_ref, wout_ref, gc_ref, wq_ref, gq_ref,
                ck_ref, cv_ref, wo_ref, o_ref):
    heads = ([om_ref[h] for h in range(MOBA_HEADS)] + [od_ref[h] for h in range(DIFF_HEADS)]
             + [os_ref[h] for h in range(DSA_HEADS)])
    mixed = jnp.concatenate(heads, axis=-1)
    x1 = x_ref[...] + jnp.dot(mixed, wout_ref[...], preferred_element_type=F32)
    hq = _rms_rows(x1, gc_ref[...]).astype(BF16)
    cq = jnp.dot(hq, wq_ref[...], preferred_element_type=F32)
    scale = HEAD_DIM ** -0.5
    outs = []
    for h in range(CROSS_HEADS):
        sl = slice(h * HEAD_DIM, (h + 1) * HEAD_DIM)
        qh = _rms_rows(cq[:, sl], gq_ref[...]).astype(BF16)
        sc = _dot_t(qh, ck_ref[:, sl]) * scale
        m = jnp.max(sc, axis=-1, keepdims=True)
        p = jnp.exp(sc - m)
        l = jnp.sum(p, axis=-1, keepdims=True)
        outs.append(jnp.dot(p.astype(BF16), cv_ref[:, sl], preferred_element_type=F32) / l)
    co = jnp.concatenate(outs, axis=-1).astype(BF16)
    o_ref[...] = x1 + jnp.dot(co, wo_ref[...], preferred_element_type=F32)


def _mid(x, om, od, osa, wout, gc, wq, gq, ck, cv, wo, s):
    tm = min(s, 512)
    n_mem = ck.shape[0]
    whole = lambda shape: pl.BlockSpec(shape, lambda i: (0,) * len(shape), pipeline_mode=pl.Buffered(1))
    heads = lambda n: pl.BlockSpec((n, tm, LANES), lambda i: (0, i, 0))
    return pl.pallas_call(
        _mid_kernel,
        grid=(s // tm,),
        in_specs=[
            pl.BlockSpec((tm, D_MODEL), lambda i: (i, 0)),
            heads(MOBA_HEADS), heads(DIFF_HEADS), heads(DSA_HEADS),
            whole((D_MODEL, D_MODEL)), whole((1, D_MODEL)), whole((D_MODEL, CROSS_W)),
            whole((1, LANES)), whole((n_mem, CROSS_W)), whole((n_mem, CROSS_W)),
            whole((CROSS_W, D_MODEL)),
        ],
        out_specs=pl.BlockSpec((tm, D_MODEL), lambda i: (i, 0)),
        out_shape=jax.ShapeDtypeStruct((s, D_MODEL), F32),
        compiler_params=_cparams(("arbitrary",)),
        name="out_proj_cross",
    )(x, om, od, osa, wout, gc, wq, gq, ck, cv, wo)


FFN_HALO = 16

def _ffn_kernel(x_ref, xp_ref, g_ref, wg_ref, wv_ref, cwg_ref, cwv_ref, cbg_ref, cbv_ref,
                wd_ref, o_ref, h_scr, *, tm):
    i = pl.program_id(0)
    f = pl.program_id(1)

    @pl.when(f == 0)
    def _():
        prev = _rms_rows(xp_ref[...], g_ref[...])
        h_scr[0:FFN_HALO, :] = jnp.where(i > 0, prev, 0.0).astype(BF16)
        h_scr[FFN_HALO:, :] = _rms_rows(x_ref[...], g_ref[...]).astype(BF16)

    h = h_scr[...]

    def conv(w_ref, cw_ref, cb_ref):
        u = jnp.dot(h, w_ref[...], preferred_element_type=F32)
        cw = cw_ref[...]
        uc = (cw[0:1] * pltpu.roll(u, 2, 0) + cw[1:2] * pltpu.roll(u, 1, 0) + cw[2:3] * u
              + cb_ref[...])
        return uc[FFN_HALO:, :]

    gate = conv(wg_ref, cwg_ref, cbg_ref)
    val = conv(wv_ref, cwv_ref, cbv_ref)
    act = (gate * jax.nn.sigmoid(gate) * val).astype(BF16)
    y = jnp.dot(act, wd_ref[...], preferred_element_type=F32)

    @pl.when(f == 0)
    def _():
        o_ref[...] = x_ref[...] + y

    @pl.when(f > 0)
    def _():
        o_ref[...] += y


def _ffn(x, g, w_up_p, cw_p, cb_p, w_down_p, s):
    tm = min(s, 512)
    nf = D_FF_PAD // FF_TILE
    halo_blocks = tm // FFN_HALO
    return pl.pallas_call(
        functools.partial(_ffn_kernel, tm=tm),
        grid=(s // tm, nf),
        in_specs=[
            pl.BlockSpec((tm, D_MODEL), lambda i, f: (i, 0)),
            pl.BlockSpec((FFN_HALO, D_MODEL), lambda i, f: (jnp.maximum(i * halo_blocks - 1, 0), 0)),
            pl.BlockSpec((1, D_MODEL), lambda i, f: (0, 0)),
            pl.BlockSpec((D_MODEL, FF_TILE), lambda i, f: (0, f)),
            pl.BlockSpec((D_MODEL, FF_TILE), lambda i, f: (0, f + nf)),
            pl.BlockSpec((3, FF_TILE), lambda i, f: (0, f)),
            pl.BlockSpec((3, FF_TILE), lambda i, f: (0, f + nf)),
            pl.BlockSpec((1, FF_TILE), lambda i, f: (0, f)),
            pl.BlockSpec((1, FF_TILE), lambda i, f: (0, f + nf)),
            pl.BlockSpec((FF_TILE, D_MODEL), lambda i, f: (f, 0)),
        ],
        out_specs=pl.BlockSpec((tm, D_MODEL), lambda i, f: (i, 0)),
        out_shape=jax.ShapeDtypeStruct((s, D_MODEL), F32),
        scratch_shapes=[pltpu.VMEM((tm + FFN_HALO, D_MODEL), BF16)],
        compiler_params=_cparams(("arbitrary", "arbitrary")),
        name="ffn",
    )(x, x, g, w_up_p, w_up_p, cw_p, cw_p, cb_p, cb_p, w_down_p)


def _pad_cols(a, width):
    return jnp.pad(a, ((0, 0), (0, width - a.shape[1])))


def _split_pad(a):
    return jnp.concatenate([_pad_cols(a[:, :D_FF], D_FF_PAD), _pad_cols(a[:, D_FF:], D_FF_PAD)], axis=1)


def kernel(x, mem, positions, attn_norm, w_in, moba_qk_gain, diff_qk_gain, diff_lambda,
           diff_subln, dsa_qk_gain, w_out, cross_norm, mem_norm, cross_wq, cross_wkv,
           cross_qk_gain, cross_wo, ffn_norm, ffn_w_up, ffn_conv_w, ffn_conv_b, ffn_w_down):
    b, s, _ = x.shape
    assert b == 1 and s % DSA_TQ == 0 and s % MOBA_BLOCK == 0
    xs = x.reshape(s, D_MODEL)
    mem2d = mem.reshape(mem.shape[1], D_MODEL)
    tabs = _rope_tables(positions, s)
    row = lambda v: v.reshape(1, -1)
    for l in range(DEPTH):
        gains = jnp.concatenate([
            moba_qk_gain[l], dsa_qk_gain[l], jnp.tile(diff_qk_gain[l], (1, 2)),
            jnp.zeros((2, LANES), F32)], axis=0)
        w_in_p = _pad_cols(w_in[l], D_IN_PAD).astype(BF16)
        main, tail = _in_proj(xs, row(attn_norm[l]), w_in_p, tabs, gains, s)
        o_moba = _moba(main, s)
        o_diff = _diff(main, diff_lambda[l], row(diff_subln[l]), l, s)
        o_dsa = _dsa(main, tail, s)
        ck, cv = _memkv(mem2d, row(mem_norm[l]), cross_wkv[l].astype(BF16), row(cross_qk_gain[l, 1]))
        xs = _mid(xs, o_moba, o_diff, o_dsa, w_out[l].astype(BF16), row(cross_norm[l]),
                  cross_wq[l].astype(BF16), row(cross_qk_gain[l, 0]), ck, cv,
                  cross_wo[l].astype(BF16), s)
        w_up_p = _split_pad(ffn_w_up[l]).astype(BF16)
        cw_p = _split_pad(ffn_conv_w[l])
        cb_p = _split_pad(row(ffn_conv_b[l]))
        w_down_p = jnp.pad(ffn_w_down[l], ((0, D_FF_PAD - D_FF), (0, 0))).astype(BF16)
        xs = _ffn(xs, row(ffn_norm[l]), w_up_p, cw_p, cb_p, w_down_p, s)
    return xs.reshape(b, s, D_MODEL)
```

```python
import functools
import math

import jax
import jax.numpy as jnp
from jax import lax
from jax.experimental import pallas as pl
from jax.experimental.pallas import tpu as pltpu

F32 = jnp.float32
BF16 = jnp.bfloat16
I32 = jnp.int32

D_MODEL = 2048
DEPTH = 2
HEAD_DIM = 128
MOBA_HEADS = 6
DIFF_HEADS = 4
DSA_HEADS = 6
MOBA_BLOCK = 256
MOBA_TOPK = 3
DIFF_QK_DIM = 64
DSA_TOPK = 256
IDX_HEADS = 8
IDX_DIM = 64
MEM_LEN = 256
CROSS_HEADS = 4
CROSS_W = CROSS_HEADS * HEAD_DIM
D_FF = 5504
ROPE_THETA = 10000.0
RMS_EPS = 1e-6

LANES = 128
PROJ_TN = 256
PROJ_ROW_CHUNK = 128
D_IN = 6728
D_IN_PAD = 6912
N_PROJ_TILES = D_IN_PAD // PROJ_TN
N_HEAD_BLOCKS = D_IN_PAD // LANES
FF_TILE = 512
D_FF_PAD = 5632
NEG = -1e30
LOG2E = 1.4426950408889634
FLASH_WIDTH = 4
MOBA_HEADS_PER_STEP = 3
DSA_HEADS_PER_STEP = 3
DIFF_HEADS_PER_STEP = 2
INT_MIN = -(2 ** 31)
VMEM_LIMIT = 56 * 1024 * 1024

BLK_MQ, BLK_MK, BLK_MV = 0, 6, 12
BLK_DQ, BLK_DK, BLK_DV = 18, 22, 26
BLK_SQ, BLK_SK, BLK_SV = 30, 36, 42
BLK_IQ, BLK_IK = 48, 52


def _cparams(sem):
    return pltpu.CompilerParams(dimension_semantics=sem, vmem_limit_bytes=VMEM_LIMIT)


def _rms_rows(x, g):
    return x * lax.rsqrt(jnp.mean(x * x, axis=-1, keepdims=True) + RMS_EPS) * g


def _dot_t(a, b):
    return lax.dot_general(a, b, (((1,), (1,)), ((), ())), preferred_element_type=F32)


def _rope_tab_kernel(pos_ref, inv128_ref, inv64_ref, c128_ref, s128_ref, c64_ref, s64_ref):
    p = pos_ref[...].astype(F32)
    lane = lax.broadcasted_iota(I32, (1, LANES), 1)
    a = p * inv128_ref[...]
    sa = jnp.sin(a)
    c128_ref[...] = jnp.cos(a)
    s128_ref[...] = jnp.where(lane < 64, -sa, sa)
    b = p * inv64_ref[...]
    sb = jnp.sin(b)
    c64_ref[...] = jnp.cos(b)
    s64_ref[...] = jnp.where((lane % 64) < 32, -sb, sb)


def _rope_tables(positions, s):
    tm = min(s, 1024)
    inv_a = ROPE_THETA ** (-(jnp.arange(64, dtype=F32) * 2.0 / 128))
    inv_b = ROPE_THETA ** (-(jnp.arange(32, dtype=F32) * 2.0 / 64))
    inv128 = jnp.tile(inv_a, 2).reshape(1, LANES)
    inv64 = jnp.tile(inv_b, 4).reshape(1, LANES)
    pos = positions.reshape(s, 1)
    tab = jax.ShapeDtypeStruct((s, LANES), F32)
    row = pl.BlockSpec((tm, LANES), lambda i: (i, 0))
    cst = pl.BlockSpec((1, LANES), lambda i: (0, 0))
    return pl.pallas_call(
        _rope_tab_kernel,
        grid=(s // tm,),
        in_specs=[pl.BlockSpec((tm, 1), lambda i: (i, 0)), cst, cst],
        out_specs=[row, row, row, row],
        out_shape=[tab, tab, tab, tab],
        compiler_params=_cparams(("arbitrary",)),
        name="rope_tables",
    )(pos, inv128, inv64)


def _proj_kernel(x_ref, g_ref, w_ref, c128_ref, s128_ref, c64_ref, s64_ref, gains_ref,
                 main_ref, tail_ref, h_scr):
    j = pl.program_id(1)

    @pl.when(j == 0)
    def _():
        h_scr[...] = _rms_rows(x_ref[...], g_ref[...]).astype(BF16)

    lane = lax.broadcasted_iota(I32, (1, LANES), 1)
    lo = lane < 64
    tm = h_scr.shape[0]
    chunks = [slice(r0, r0 + PROJ_ROW_CHUNK) for r0 in range(0, tm, PROJ_ROW_CHUNK)]

    def product(rs):
        return jnp.dot(h_scr[rs, :], w_ref[...], preferred_element_type=F32)

    def rope128(y, rs):
        return y * c128_ref[rs, :] + pltpu.roll(y, 64, 1) * s128_ref[rs, :]

    def rope64(y, rs):
        partner = jnp.where((lane % 64) < 32, pltpu.roll(y, 96, 1), pltpu.roll(y, 32, 1))
        return y * c64_ref[rs, :] + partner * s64_ref[rs, :]

    def norm64(y, g):
        y2 = y * y
        s_lo = jnp.sum(jnp.where(lo, y2, 0.0), axis=-1, keepdims=True)
        s_hi = jnp.sum(jnp.where(lo, 0.0, y2), axis=-1, keepdims=True)
        ms = jnp.where(lo, s_lo, s_hi) * (1.0 / 64)
        return y * lax.rsqrt(ms + RMS_EPS) * g

    def emit(fn):
        for rs in chunks:
            r = product(rs)
            for c in range(2):
                main_ref[c, rs, :] = fn(r[:, c * LANES:(c + 1) * LANES], rs).astype(BF16)

    def qk128(row, post=None):
        fn = lambda y, rs: rope128(_rms_rows(y, gains_ref[row:row + 1, :]), rs)
        return fn if post is None else (lambda y, rs: fn(y, rs) * post)

    def qk64(row, post=None):
        fn = lambda y, rs: rope64(norm64(y, gains_ref[row:row + 1, :]), rs)
        return fn if post is None else (lambda y, rs: fn(y, rs) * post)

    ident = lambda y, rs: y
    q128 = HEAD_DIM ** -0.5 * LOG2E
    q64 = DIFF_QK_DIM ** -0.5 * LOG2E
    segments = (
        (0, 3, qk128(0, q128)), (3, 6, qk128(1)), (6, 9, ident),
        (9, 11, qk64(4, q64)), (11, 13, qk64(5)), (13, 15, ident),
        (15, 18, qk128(2, q128)), (18, 21, qk128(3)), (21, 24, ident),
        (24, 26, rope64),
    )
    for first, last, fn in segments:
        pl.when((j >= first) & (j < last))(functools.partial(emit, fn))

    @pl.when(j == N_PROJ_TILES - 1)
    def _():
        for rs in chunks:
            y = product(rs)[:, :LANES]
            kk = jnp.where(lo, rope64(y, rs), 0.0)
            main_ref[0, rs, :] = kk.astype(BF16)
            main_ref[1, rs, :] = pltpu.roll(kk, 64, 1).astype(BF16)
            tail_ref[rs, :] = (y * (IDX_HEADS ** -0.5)) * (IDX_DIM ** -0.5)


def _in_proj(x, g, w_in_p, tabs, gains, s):
    tm = min(s, 512)
    c128, s128, c64, s64 = tabs
    row = pl.BlockSpec((tm, LANES), lambda i, j: (i, 0))
    return pl.pallas_call(
        _proj_kernel,
        grid=(s // tm, N_PROJ_TILES),
        in_specs=[
            pl.BlockSpec((tm, D_MODEL), lambda i, j: (i, 0)),
            pl.BlockSpec((1, D_MODEL), lambda i, j: (0, 0)),
            pl.BlockSpec((D_MODEL, PROJ_TN), lambda i, j: (0, j)),
            row, row, row, row,
            pl.BlockSpec((8, LANES), lambda i, j: (0, 0)),
        ],
        out_specs=[
            pl.BlockSpec((2, tm, LANES), lambda i, j: (j, i, 0)),
            pl.BlockSpec((tm, LANES), lambda i, j: (i, 0)),
        ],
        out_shape=[
            jax.ShapeDtypeStruct((N_HEAD_BLOCKS, s, LANES), BF16),
            jax.ShapeDtypeStruct((s, LANES), F32),
        ],
        scratch_shapes=[pltpu.VMEM((tm, D_MODEL), BF16)],
        compiler_params=_cparams(("arbitrary", "arbitrary")),
        name="in_proj",
    )(x, g, w_in_p, c128, s128, c64, s64, gains)


def _flash_slab(q, k_slab, v_slab, masks, carry):
    m, l, acc = carry
    sc = _dot_t(q, k_slab)
    if masks is not None:
        tk = sc.shape[1] // len(masks)
        sc = jnp.concatenate([jnp.where(mk, sc[:, u * tk:(u + 1) * tk], NEG)
                              for u, mk in enumerate(masks)], axis=1)
    m_new = jnp.maximum(m, jnp.max(sc, axis=-1, keepdims=True))
    p = jnp.exp2(sc - m_new)
    alpha = jnp.exp2(m - m_new)
    l = alpha * l + jnp.sum(p, axis=-1, keepdims=True)
    acc = alpha * acc + jnp.dot(p.astype(BF16), v_slab, preferred_element_type=F32)
    return m_new, l, acc


def _flash_init(rows):
    return (jnp.full((rows, 1), NEG, F32), jnp.zeros((rows, 1), F32),
            jnp.zeros((rows, HEAD_DIM), F32))


def _flash_loop(streams, first_slab, n_slabs, states, shared_masks=None):
    def slab(g, sts):
        common = None if shared_masks is None else shared_masks(g)
        out = []
        for (q, load_slab), st in zip(streams, sts):
            k_slab, v_slab, masks = load_slab(g)
            out.append(_flash_slab(q, k_slab, v_slab, masks if common is None else common, st))
        return tuple(out)
    return lax.fori_loop(first_slab, n_slabs, slab, tuple(states))


def _flash_finish(state):
    _, l, acc = state
    return acc / l


def _moba_kernel(q_ref, k_ref, v_ref, o_ref, kbar_scr, *, nb, n_sel):
    blk = MOBA_BLOCK
    i = pl.program_id(1)

    @pl.when(i == 0)
    def _():
        kbar_scr[...] = jnp.zeros_like(kbar_scr)
        for hh in range(MOBA_HEADS_PER_STEP):
            def mean_block(n, c, hh=hh):
                kk = k_ref[hh, pl.ds(pl.multiple_of(n * blk, blk), blk), :].astype(F32)
                kbar_scr[hh, pl.ds(n, 1), :] = jnp.sum(kk, axis=0, keepdims=True) * (1.0 / blk)
                return c
            lax.fori_loop(0, nb, mean_block, 0)

    blk_id = lax.broadcasted_iota(I32, (LANES, blk), 0)
    valid = blk_id < i
    bl = lax.broadcasted_iota(I32, (blk, LANES), 1)

    def stream(hh):
        q = q_ref[hh]
        gate = _dot_t(kbar_scr[hh].astype(BF16), q)
        g = jnp.where(valid, gate, -jnp.inf)
        sel_t = jnp.zeros((LANES, blk), F32)
        for _ in range(n_sel):
            mx = jnp.max(g, axis=0, keepdims=True)
            first = jnp.min(jnp.where(g == mx, blk_id, LANES), axis=0, keepdims=True)
            hit = blk_id == first
            sel_t = jnp.where(hit, 1.0, sel_t)
            g = jnp.where(hit, -jnp.inf, g)
        sel = jnp.where(valid, sel_t, 0.0).T

        lim_tab = jnp.where(bl == i, row, jnp.where(sel > 0.0, blk - 1.0, -1.0))

        def load(g):
            masks = []
            for u in range(FLASH_WIDTH):
                j = g * FLASH_WIDTH + u
                lim = jnp.max(jnp.where(bl == j, lim_tab, -1.0), axis=-1, keepdims=True)
                masks.append(cols <= lim)
            sl = pl.ds(pl.multiple_of(g * slab_keys, slab_keys), slab_keys)
            return k_ref[hh, sl, :], v_ref[hh, sl, :], masks
        return q, load

    row = lax.broadcasted_iota(I32, (blk, LANES), 0).astype(F32)
    cols = lax.broadcasted_iota(I32, (blk, blk), 1).astype(F32)
    slab_keys = FLASH_WIDTH * blk
    streams = [stream(hh) for hh in range(MOBA_HEADS_PER_STEP)]
    n_slabs = (i + FLASH_WIDTH) // FLASH_WIDTH
    states = _flash_loop(streams, 0, n_slabs, [_flash_init(blk) for _ in streams])
    for hh, st in enumerate(states):
        o_ref[hh] = _flash_finish(st).astype(BF16)


def _moba(main, s):
    blk = MOBA_BLOCK
    nb = s // blk
    hps = MOBA_HEADS_PER_STEP
    assert nb <= LANES and nb % FLASH_WIDTH == 0 and MOBA_HEADS % hps == 0
    n_sel = max(1, min(MOBA_TOPK, nb - 1))
    return pl.pallas_call(
        functools.partial(_moba_kernel, nb=nb, n_sel=n_sel),
        grid=(MOBA_HEADS // hps, nb),
        in_specs=[
            pl.BlockSpec((hps, blk, LANES), lambda h, i: (BLK_MQ // hps + h, i, 0)),
            pl.BlockSpec((hps, s, LANES), lambda h, i: (BLK_MK // hps + h, 0, 0)),
            pl.BlockSpec((hps, s, LANES), lambda h, i: (BLK_MV // hps + h, 0, 0)),
        ],
        out_specs=pl.BlockSpec((hps, blk, LANES), lambda h, i: (h, i, 0)),
        out_shape=jax.ShapeDtypeStruct((MOBA_HEADS, s, LANES), BF16),
        scratch_shapes=[pltpu.VMEM((hps, LANES, LANES), F32)],
        compiler_params=_cparams(("arbitrary", "arbitrary")),
        name="moba",
    )(main, main, main)


def _diff_kernel(q_ref, k_ref, v_ref, lam_ref, g_ref, o_ref, *, tq, lam_init):
    i = pl.program_id(1)
    lane = lax.broadcasted_iota(I32, (tq, LANES), 1)
    slab_keys = FLASH_WIDTH * tq
    row = lax.broadcasted_iota(I32, (2 * tq, 1), 0) & (tq - 1)
    cols = lax.broadcasted_iota(I32, (2 * tq, tq), 1)

    def tail_masks(g):
        masks = []
        for u in range(FLASH_WIDTH):
            j = g * FLASH_WIDTH + u
            masks.append(cols <= jnp.where(j < i, tq - 1, jnp.where(j == i, row, -1)))
        return masks

    def stream(hh):
        q = q_ref[hh]
        zero = jnp.zeros_like(q)
        qq = jnp.concatenate([jnp.where(lane < 64, q, zero), jnp.where(lane < 64, zero, q)], axis=0)

        def load(g):
            sl = pl.ds(pl.multiple_of(g * slab_keys, slab_keys), slab_keys)
            return k_ref[hh, sl, :], v_ref[hh, sl, :], None
        return qq, load

    streams = [stream(hh) for hh in range(DIFF_HEADS_PER_STEP)]
    full = i // FLASH_WIDTH
    states = _flash_loop(streams, 0, full, [_flash_init(2 * tq) for _ in streams])
    states = _flash_loop(streams, full, full + 1, states, shared_masks=tail_masks)
    lf = lam_ref[...]
    lam = (jnp.exp(jnp.sum(lf[0:1] * lf[1:2], axis=-1, keepdims=True))
           - jnp.exp(jnp.sum(lf[2:3] * lf[3:4], axis=-1, keepdims=True)) + lam_init)
    for hh, st in enumerate(states):
        o = _flash_finish(st)
        a = o[:tq] - lam * o[tq:]
        o_ref[hh] = (_rms_rows(a, g_ref[...]) * (1.0 - lam_init)).astype(BF16)


def _diff(main, lam_params, subln, layer, s):
    tq = min(s, 256)
    hps = DIFF_HEADS_PER_STEP
    assert (s // tq) % FLASH_WIDTH == 0 and DIFF_HEADS % hps == 0
    lam_init = 0.8 - 0.6 * math.exp(-0.3 * layer)
    return pl.pallas_call(
        functools.partial(_diff_kernel, tq=tq, lam_init=lam_init),
        grid=(DIFF_HEADS // hps, s // tq),
        in_specs=[
            pl.BlockSpec((hps, tq, LANES), lambda h, i: (BLK_DQ // hps + h, i, 0)),
            pl.BlockSpec((hps, s, LANES), lambda h, i: (BLK_DK // hps + h, 0, 0)),
            pl.BlockSpec((hps, s, LANES), lambda h, i: (BLK_DV // hps + h, 0, 0)),
            pl.BlockSpec((4, DIFF_QK_DIM), lambda h, i: (0, 0)),
            pl.BlockSpec((1, LANES), lambda h, i: (0, 0)),
        ],
        out_specs=pl.BlockSpec((hps, tq, LANES), lambda h, i: (h, i, 0)),
        out_shape=jax.ShapeDtypeStruct((DIFF_HEADS, s, LANES), BF16),
        compiler_params=_cparams(("arbitrary", "arbitrary")),
        name="diff_attn",
    )(main, main, main, lam_params, subln)


DSA_TQ = 256


def _sort_key(sc):
    sc = jnp.where(sc == 0.0, 0.0, sc)
    b = lax.bitcast_convert_type(sc, I32)
    return b ^ ((b >> 31) & 0x7FFFFFFF)


def _dsa_kernel(qi_ref, ki_ref, w_ref, q_ref, k_ref, v_ref, o_ref,
                key_scr, keyt_scr, wb_scr, thr_scr, *, n_keep):
    t = DSA_TQ
    i = pl.program_id(0)
    w = w_ref[...]
    for h in range(IDX_HEADS):
        wb_scr[h] = jnp.broadcast_to(w[:, 64 + h:65 + h], (t, t))

    def tile(j):
        return pl.ds(pl.multiple_of(j * t, t), t)

    q_idx = qi_ref[...].reshape(IDX_HEADS // 2 * t, LANES)

    def index_scores(j):
        s_even = _dot_t(q_idx, ki_ref[0, tile(j), :])
        s_odd = _dot_t(q_idx, ki_ref[1, tile(j), :])
        sc = jnp.zeros((t, t), F32)
        for b in range(IDX_HEADS // 2):
            rs = slice(b * t, (b + 1) * t)
            sc = sc + wb_scr[2 * b] * jnp.maximum(s_even[rs], 0.0)
            sc = sc + wb_scr[2 * b + 1] * jnp.maximum(s_odd[rs], 0.0)
        return sc

    rows = lax.broadcasted_iota(I32, (t, t), 0)
    cols = lax.broadcasted_iota(I32, (t, t), 1)
    causal = cols <= rows

    def put_keys(j, keys):
        key_scr[:, tile(j)] = keys
        keyt_scr[tile(j), :] = keys.T

    def fill_past(j, c):
        put_keys(j, _sort_key(index_scores(j)))
        return c
    lax.fori_loop(0, i, fill_past, 0)
    put_keys(i, jnp.where(causal, _sort_key(index_scores(i)), INT_MIN))
    n_steps = (i + FLASH_WIDTH) // FLASH_WIDTH

    def fill_pad(j, c):
        key_scr[:, tile(j)] = jnp.full((t, t), INT_MIN, I32)
        return c
    lax.fori_loop(i + 1, n_steps * FLASH_WIDTH, fill_pad, 0)

    def count_keys(preds):
        def step(j, cnts):
            kk = keyt_scr[tile(j), :].reshape(t // 8, 8, t)
            return tuple(c + jnp.sum(jnp.where(p(kk), 1, 0), axis=0) for c, p in zip(cnts, preds))

        def pair(jj, cnts):
            return step(2 * jj + 1, step(2 * jj, cnts))
        zero = jnp.zeros((8, t), I32)
        n_pairs = (i + 1) // 2
        cnts = lax.fori_loop(0, n_pairs, pair, tuple(zero for _ in preds))
        cnts = lax.fori_loop(2 * n_pairs, i + 1, step, cnts)
        return [jnp.sum(c, axis=0, keepdims=True) for c in cnts]

    def bit_body(b, prefix):
        cand_u = prefix | lax.shift_left(jnp.int32(1), 31 - b)
        cand = cand_u ^ INT_MIN
        total, = count_keys([lambda kk: kk >= cand])
        return jnp.where(total >= n_keep, cand_u, prefix)

    prefix = lax.fori_loop(0, 32, bit_body, jnp.zeros((8, t), I32))
    thr_t = jnp.maximum(prefix ^ INT_MIN, INT_MIN + 1)
    thr_scr[...] = jnp.broadcast_to(thr_t[0:1], (t, t)).T

    n_gt_t, n_ge_t = count_keys([lambda kk: kk > thr_t, lambda kk: kk >= thr_t])
    surplus = jnp.max(jnp.where(n_ge_t > n_keep, 1, 0))

    @pl.when(surplus > 0)
    def _():
        need_t = (n_keep - n_gt_t).astype(F32)
        need = jnp.broadcast_to(need_t, (t, t)).T[:, 0:1]
        thr = thr_scr[...]
        upper = jnp.where(rows <= cols, 1.0, 0.0).astype(BF16)

        def drop_late_ties(j, before):
            kk = key_scr[:, tile(j)]
            tie = kk == thr
            seen = before + jnp.dot(jnp.where(tie, 1.0, 0.0).astype(BF16), upper,
                                    preferred_element_type=F32)
            key_scr[:, tile(j)] = jnp.where(tie, jnp.where(seen > need, INT_MIN, kk), kk)
            return seen[:, t - 1:t]
        lax.fori_loop(0, i + 1, drop_late_ties, jnp.zeros((t, 1), F32))

    slab_keys = FLASH_WIDTH * t

    def admitted(g):
        return [key_scr[:, tile(g * FLASH_WIDTH + u)] >= thr_scr[...] for u in range(FLASH_WIDTH)]

    def head_group(hg, c):
        def stream(h):
            def load(g):
                sl = pl.ds(pl.multiple_of(g * slab_keys, slab_keys), slab_keys)
                return k_ref[h, sl, :], v_ref[h, sl, :], None
            return q_ref[h], load

        heads = [hg * DSA_HEADS_PER_STEP + u for u in range(DSA_HEADS_PER_STEP)]
        states = _flash_loop([stream(h) for h in heads], 0, n_steps, [_flash_init(t) for _ in heads],
                             shared_masks=admitted)
        for h, st in zip(heads, states):
            o_ref[h] = _flash_finish(st).astype(BF16)
        return c
    lax.fori_loop(0, DSA_HEADS // DSA_HEADS_PER_STEP, head_group, 0)


def _dsa(main, tail, s):
    t = DSA_TQ
    n_keep = min(DSA_TOPK, s // 4)
    slab = FLASH_WIDTH * t
    s_pad = -(-s // slab) * slab
    resident = dict(pipeline_mode=pl.Buffered(1))
    return pl.pallas_call(
        functools.partial(_dsa_kernel, n_keep=n_keep),
        grid=(s // t,),
        in_specs=[
            pl.BlockSpec((IDX_HEADS // 2, t, LANES), lambda i: (BLK_IQ // 4, i, 0)),
            pl.BlockSpec((2, s, LANES), lambda i: (BLK_IK // 2, 0, 0), **resident),
            pl.BlockSpec((t, LANES), lambda i: (i, 0)),
            pl.BlockSpec((DSA_HEADS, t, LANES), lambda i: (BLK_SQ // 6, i, 0)),
            pl.BlockSpec((DSA_HEADS, s, LANES), lambda i: (BLK_SK // 6, 0, 0), **resident),
            pl.BlockSpec((DSA_HEADS, s, LANES), lambda i: (BLK_SV // 6, 0, 0), **resident),
        ],
        out_specs=pl.BlockSpec((DSA_HEADS, t, LANES), lambda i: (0, i, 0)),
        out_shape=jax.ShapeDtypeStruct((DSA_HEADS, s, LANES), BF16),
        scratch_shapes=[
            pltpu.VMEM((t, s_pad), I32),
            pltpu.VMEM((s, t), I32),
            pltpu.VMEM((IDX_HEADS, t, t), F32),
            pltpu.VMEM((t, t), I32),
        ],
        compiler_params=_cparams(("arbitrary",)),
        name="dsa",
    )(main, main, tail, main, main, main)


def _memkv_kernel(mem_ref, g_ref, wkv_ref, gk_ref, ck_ref, cv_ref):
    m = _rms_rows(mem_ref[...], g_ref[...]).astype(BF16)
    kv = jnp.dot(m, wkv_ref[...], preferred_element_type=F32)
    for h in range(CROSS_HEADS):
        sl = slice(h * HEAD_DIM, (h + 1) * HEAD_DIM)
        ck_ref[:, sl] = _rms_rows(kv[:, sl], gk_ref[...]).astype(BF16)
    cv_ref[...] = kv[:, CROSS_W:].astype(BF16)


def _memkv(mem2d, g, wkv, gk):
    n = mem2d.shape[0]
    out = jax.ShapeDtypeStruct((n, CROSS_W), BF16)
    return pl.pallas_call(
        _memkv_kernel,
        out_shape=[out, out],
        compiler_params=pltpu.CompilerParams(vmem_limit_bytes=VMEM_LIMIT),
        name="mem_kv",
    )(mem2d, g, wkv, gk)


def _mid_kernel(x_ref, om_ref, od_ref, os_ref, wout_ref, gc_ref, wq_ref, gq_ref,
                ck_ref, cv_ref, wo_ref, o_ref):
    heads = ([om_ref[h] for h in range(MOBA_HEADS)] + [od_ref[h] for h in range(DIFF_HEADS)]
             + [os_ref[h] for h in range(DSA_HEADS)])
    mixed = jnp.concatenate(heads, axis=-1)
    x1 = x_ref[...] + jnp.dot(mixed, wout_ref[...], preferred_element_type=F32)
    hq = _rms_rows(x1, gc_ref[...]).astype(BF16)
    cq = jnp.dot(hq, wq_ref[...], preferred_element_type=F32)
    scale = HEAD_DIM ** -0.5
    outs = []
    for h in range(CROSS_HEADS):
        sl = slice(h * HEAD_DIM, (h + 1) * HEAD_DIM)
        qh = _rms_rows(cq[:, sl], gq_ref[...]).astype(BF16)
        sc = _dot_t(qh, ck_ref[:, sl]) * scale
        m = jnp.max(sc, axis=-1, keepdims=True)
        p = jnp.exp(sc - m)
        l = jnp.sum(p, axis=-1, keepdims=True)
        outs.append(jnp.dot(p.astype(BF16), cv_ref[:, sl], preferred_element_type=F32) / l)
    co = jnp.concatenate(outs, axis=-1).astype(BF16)
    o_ref[...] = x1 + jnp.dot(co, wo_ref[...], preferred_element_type=F32)


def _mid(x, om, od, osa, wout, gc, wq, gq, ck, cv, wo, s):
    tm = min(s, 512)
    n_mem = ck.shape[0]
    whole = lambda shape: pl.BlockSpec(shape, lambda i: (0,) * len(shape), pipeline_mode=pl.Buffered(1))
    heads = lambda n: pl.BlockSpec((n, tm, LANES), lambda i: (0, i, 0))
    return pl.pallas_call(
        _mid_kernel,
        grid=(s // tm,),
        in_specs=[
            pl.BlockSpec((tm, D_MODEL), lambda i: (i, 0)),
            heads(MOBA_HEADS), heads(DIFF_HEADS), heads(DSA_HEADS),
            whole((D_MODEL, D_MODEL)), whole((1, D_MODEL)), whole((D_MODEL, CROSS_W)),
            whole((1, LANES)), whole((n_mem, CROSS_W)), whole((n_mem, CROSS_W)),
            whole((CROSS_W, D_MODEL)),
        ],
        out_specs=pl.BlockSpec((tm, D_MODEL), lambda i: (i, 0)),
        out_shape=jax.ShapeDtypeStruct((s, D_MODEL), F32),
        compiler_params=_cparams(("arbitrary",)),
        name="out_proj_cross",
    )(x, om, od, osa, wout, gc, wq, gq, ck, cv, wo)


FFN_HALO = 16
FFN_ROW_CHUNK = 256

def _ffn_kernel(x_ref, xp_ref, g_ref, wg_ref, wv_ref, cwg_ref, cwv_ref, cbg_ref, cbv_ref,
                wd_ref, o_ref, h_scr, *, tm):
    i = pl.program_id(0)
    f = pl.program_id(1)

    @pl.when(f == 0)
    def _():
        prev = _rms_rows(xp_ref[...], g_ref[...])
        h_scr[0:FFN_HALO, :] = jnp.where(i > 0, prev, 0.0).astype(BF16)
        h_scr[FFN_HALO:, :] = _rms_rows(x_ref[...], g_ref[...]).astype(BF16)
        o_ref[...] = x_ref[...]

    for r0 in range(0, tm, FFN_ROW_CHUNK):
        h = h_scr[r0:r0 + FFN_ROW_CHUNK + FFN_HALO, :]

        def conv(w_ref, cw_ref, cb_ref):
            u = jnp.dot(h, w_ref[...], preferred_element_type=F32)
            cw = cw_ref[...]
            uc = (cw[0:1] * pltpu.roll(u, 2, 0) + cw[1:2] * pltpu.roll(u, 1, 0) + cw[2:3] * u
                  + cb_ref[...])
            return uc[FFN_HALO:, :]

        gate = conv(wg_ref, cwg_ref, cbg_ref)
        val = conv(wv_ref, cwv_ref, cbv_ref)
        act = (gate * jax.nn.sigmoid(gate) * val).astype(BF16)
        o_ref[r0:r0 + FFN_ROW_CHUNK, :] += jnp.dot(act, wd_ref[...], preferred_element_type=F32)


def _ffn(x, g, w_up_p, cw_p, cb_p, w_down_p, s):
    tm = min(s, 512)
    nf = D_FF_PAD // FF_TILE
    halo_blocks = tm // FFN_HALO
    return pl.pallas_call(
        functools.partial(_ffn_kernel, tm=tm),
        grid=(s // tm, nf),
        in_specs=[
            pl.BlockSpec((tm, D_MODEL), lambda i, f: (i, 0)),
            pl.BlockSpec((FFN_HALO, D_MODEL), lambda i, f: (jnp.maximum(i * halo_blocks - 1, 0), 0)),
            pl.BlockSpec((1, D_MODEL), lambda i, f: (0, 0)),
            pl.BlockSpec((D_MODEL, FF_TILE), lambda i, f: (0, f)),
            pl.BlockSpec((D_MODEL, FF_TILE), lambda i, f: (0, f + nf)),
            pl.BlockSpec((3, FF_TILE), lambda i, f: (0, f)),
            pl.BlockSpec((3, FF_TILE), lambda i, f: (0, f + nf)),
            pl.BlockSpec((1, FF_TILE), lambda i, f: (0, f)),
            pl.BlockSpec((1, FF_TILE), lambda i, f: (0, f + nf)),
            pl.BlockSpec((FF_TILE, D_MODEL), lambda i, f: (f, 0)),
        ],
        out_specs=pl.BlockSpec((tm, D_MODEL), lambda i, f: (i, 0)),
        out_shape=jax.ShapeDtypeStruct((s, D_MODEL), F32),
        scratch_shapes=[pltpu.VMEM((tm + FFN_HALO, D_MODEL), BF16)],
        compiler_params=_cparams(("arbitrary", "arbitrary")),
        name="ffn",
    )(x, x, g, w_up_p, w_up_p, cw_p, cw_p, cb_p, cb_p, w_down_p)


def _pad_cols(a, width):
    return jnp.pad(a, ((0, 0), (0, width - a.shape[1])))


def _split_pad(a):
    return jnp.concatenate([_pad_cols(a[:, :D_FF], D_FF_PAD), _pad_cols(a[:, D_FF:], D_FF_PAD)], axis=1)


def kernel(x, mem, positions, attn_norm, w_in, moba_qk_gain, diff_qk_gain, diff_lambda,
           diff_subln, dsa_qk_gain, w_out, cross_norm, mem_norm, cross_wq, cross_wkv,
           cross_qk_gain, cross_wo, ffn_norm, ffn_w_up, ffn_conv_w, ffn_conv_b, ffn_w_down):
    b, s, _ = x.shape
    assert b == 1 and s % DSA_TQ == 0 and s % MOBA_BLOCK == 0
    xs = x.reshape(s, D_MODEL)
    mem2d = mem.reshape(mem.shape[1], D_MODEL)
    tabs = _rope_tables(positions, s)
    row = lambda v: v.reshape(1, -1)
    for l in range(DEPTH):
        gains = jnp.concatenate([
            moba_qk_gain[l], dsa_qk_gain[l], jnp.tile(diff_qk_gain[l], (1, 2)),
            jnp.zeros((2, LANES), F32)], axis=0)
        w_in_p = _pad_cols(w_in[l], D_IN_PAD).astype(BF16)
        main, tail = _in_proj(xs, row(attn_norm[l]), w_in_p, tabs, gains, s)
        o_moba = _moba(main, s)
        o_diff = _diff(main, diff_lambda[l], row(diff_subln[l]), l, s)
        o_dsa = _dsa(main, tail, s)
        ck, cv = _memkv(mem2d, row(mem_norm[l]), cross_wkv[l].astype(BF16), row(cross_qk_gain[l, 1]))
        xs = _mid(xs, o_moba, o_diff, o_dsa, w_out[l].astype(BF16), row(cross_norm[l]),
                  cross_wq[l].astype(BF16), row(cross_qk_gain[l, 0]), ck, cv,
                  cross_wo[l].astype(BF16), s)
        w_up_p = _split_pad(ffn_w_up[l]).astype(BF16)
        cw_p = _split_pad(ffn_conv_w[l])
        cb_p = _split_pad(row(ffn_conv_b[l]))
        w_down_p = jnp.pad(ffn_w_down[l], ((0, D_FF_PAD - D_FF), (0, 0))).astype(BF16)
        xs = _ffn(xs, row(ffn_norm[l]), w_up_p, cw_p, cb_p, w_down_p, s)
    return xs.reshape(b, s, D_MODEL)
```

```python
import functools
import math

import jax
import jax.numpy as jnp
from jax import lax
from jax.experimental import pallas as pl
from jax.experimental.pallas import tpu as pltpu

F32 = jnp.float32
BF16 = jnp.bfloat16
I32 = jnp.int32

D_MODEL = 2048
DEPTH = 2
HEAD_DIM = 128
MOBA_HEADS = 6
DIFF_HEADS = 4
DSA_HEADS = 6
MOBA_BLOCK = 256
MOBA_TOPK = 3
DIFF_QK_DIM = 64
DSA_TOPK = 256
IDX_HEADS = 8
IDX_DIM = 64
MEM_LEN = 256
CROSS_HEADS = 4
CROSS_W = CROSS_HEADS * HEAD_DIM
D_FF = 5504
ROPE_THETA = 10000.0
RMS_EPS = 1e-6

LANES = 128
PROJ_TN = 256
PROJ_ROW_CHUNK = 128
D_IN = 6728
D_IN_PAD = 6912
N_PROJ_TILES = D_IN_PAD // PROJ_TN
N_HEAD_BLOCKS = D_IN_PAD // LANES
FF_TILE = 512
D_FF_PAD = 5632
NEG = -1e30
LOG2E = 1.4426950408889634
FLASH_WIDTH = 4
MOBA_HEADS_PER_STEP = 3
DSA_HEADS_PER_STEP = 3
DIFF_HEADS_PER_STEP = 2
INT_MIN = -(2 ** 31)
VMEM_LIMIT = 56 * 1024 * 1024

BLK_MQ, BLK_MK, BLK_MV = 0, 6, 12
BLK_DQ, BLK_DK, BLK_DV = 18, 22, 26
BLK_SQ, BLK_SK, BLK_SV = 30, 36, 42
BLK_IQ, BLK_IK = 48, 52


def _cparams(sem):
    return pltpu.CompilerParams(dimension_semantics=sem, vmem_limit_bytes=VMEM_LIMIT)


def _rms_rows(x, g):
    return x * lax.rsqrt(jnp.mean(x * x, axis=-1, keepdims=True) + RMS_EPS) * g


def _dot_t(a, b):
    return lax.dot_general(a, b, (((1,), (1,)), ((), ())), preferred_element_type=F32)


def _rope_tab_kernel(pos_ref, inv128_ref, inv64_ref, c128_ref, s128_ref, c64_ref, s64_ref):
    p = pos_ref[...].astype(F32)
    lane = lax.broadcasted_iota(I32, (1, LANES), 1)
    a = p * inv128_ref[...]
    sa = jnp.sin(a)
    c128_ref[...] = jnp.cos(a)
    s128_ref[...] = jnp.where(lane < 64, -sa, sa)
    b = p * inv64_ref[...]
    sb = jnp.sin(b)
    c64_ref[...] = jnp.cos(b)
    s64_ref[...] = jnp.where((lane % 64) < 32, -sb, sb)


def _rope_tables(positions, s):
    tm = min(s, 1024)
    inv_a = ROPE_THETA ** (-(jnp.arange(64, dtype=F32) * 2.0 / 128))
    inv_b = ROPE_THETA ** (-(jnp.arange(32, dtype=F32) * 2.0 / 64))
    inv128 = jnp.tile(inv_a, 2).reshape(1, LANES)
    inv64 = jnp.tile(inv_b, 4).reshape(1, LANES)
    pos = positions.reshape(s, 1)
    tab = jax.ShapeDtypeStruct((s, LANES), F32)
    row = pl.BlockSpec((tm, LANES), lambda i: (i, 0))
    cst = pl.BlockSpec((1, LANES), lambda i: (0, 0))
    return pl.pallas_call(
        _rope_tab_kernel,
        grid=(s // tm,),
        in_specs=[pl.BlockSpec((tm, 1), lambda i: (i, 0)), cst, cst],
        out_specs=[row, row, row, row],
        out_shape=[tab, tab, tab, tab],
        compiler_params=_cparams(("arbitrary",)),
        name="rope_tables",
    )(pos, inv128, inv64)


def _proj_kernel(x_ref, g_ref, w_ref, c128_ref, s128_ref, c64_ref, s64_ref, gains_ref,
                 main_ref, tail_ref, h_scr):
    j = pl.program_id(1)

    @pl.when(j == 0)
    def _():
        h_scr[...] = _rms_rows(x_ref[...], g_ref[...]).astype(BF16)

    lane = lax.broadcasted_iota(I32, (1, LANES), 1)
    lo = lane < 64
    tm = h_scr.shape[0]
    chunks = [slice(r0, r0 + PROJ_ROW_CHUNK) for r0 in range(0, tm, PROJ_ROW_CHUNK)]

    def product(rs):
        return jnp.dot(h_scr[rs, :], w_ref[0], preferred_element_type=F32)

    def rope128(y, rs):
        return y * c128_ref[rs, :] + pltpu.roll(y, 64, 1) * s128_ref[rs, :]

    def rope64(y, rs):
        partner = jnp.where((lane % 64) < 32, pltpu.roll(y, 96, 1), pltpu.roll(y, 32, 1))
        return y * c64_ref[rs, :] + partner * s64_ref[rs, :]

    def norm64(y, g):
        y2 = y * y
        s_lo = jnp.sum(jnp.where(lo, y2, 0.0), axis=-1, keepdims=True)
        s_hi = jnp.sum(jnp.where(lo, 0.0, y2), axis=-1, keepdims=True)
        ms = jnp.where(lo, s_lo, s_hi) * (1.0 / 64)
        return y * lax.rsqrt(ms + RMS_EPS) * g

    def emit(fn):
        for rs in chunks:
            r = product(rs)
            for c in range(2):
                main_ref[c, rs, :] = fn(r[:, c * LANES:(c + 1) * LANES], rs).astype(BF16)

    def qk128(row, post=None):
        fn = lambda y, rs: rope128(_rms_rows(y, gains_ref[row:row + 1, :]), rs)
        return fn if post is None else (lambda y, rs: fn(y, rs) * post)

    def qk64(row, post=None):
        fn = lambda y, rs: rope64(norm64(y, gains_ref[row:row + 1, :]), rs)
        return fn if post is None else (lambda y, rs: fn(y, rs) * post)

    ident = lambda y, rs: y
    q128 = HEAD_DIM ** -0.5 * LOG2E
    q64 = DIFF_QK_DIM ** -0.5 * LOG2E
    segments = (
        (0, 3, qk128(0, q128)), (3, 6, qk128(1)), (6, 9, ident),
        (9, 11, qk64(4, q64)), (11, 13, qk64(5)), (13, 15, ident),
        (15, 18, qk128(2, q128)), (18, 21, qk128(3)), (21, 24, ident),
        (24, 26, rope64),
    )
    for first, last, fn in segments:
        pl.when((j >= first) & (j < last))(functools.partial(emit, fn))

    @pl.when(j == N_PROJ_TILES - 1)
    def _():
        for rs in chunks:
            y = product(rs)[:, :LANES]
            kk = jnp.where(lo, rope64(y, rs), 0.0)
            main_ref[0, rs, :] = kk.astype(BF16)
            main_ref[1, rs, :] = pltpu.roll(kk, 64, 1).astype(BF16)
            tail_ref[rs, :] = (y * (IDX_HEADS ** -0.5)) * (IDX_DIM ** -0.5)


def _in_proj(x, g, w_in_p, tabs, gains, s):
    tm = min(s, 512)
    c128, s128, c64, s64 = tabs
    row = pl.BlockSpec((tm, LANES), lambda i, j: (i, 0))
    return pl.pallas_call(
        _proj_kernel,
        grid=(s // tm, N_PROJ_TILES),
        in_specs=[
            pl.BlockSpec((tm, D_MODEL), lambda i, j: (i, 0)),
            pl.BlockSpec((1, D_MODEL), lambda i, j: (0, 0)),
            pl.BlockSpec((1, D_MODEL, PROJ_TN), lambda i, j: (j, 0, 0)),
            row, row, row, row,
            pl.BlockSpec((8, LANES), lambda i, j: (0, 0)),
        ],
        out_specs=[
            pl.BlockSpec((2, tm, LANES), lambda i, j: (j, i, 0)),
            pl.BlockSpec((tm, LANES), lambda i, j: (i, 0)),
        ],
        out_shape=[
            jax.ShapeDtypeStruct((N_HEAD_BLOCKS, s, LANES), BF16),
            jax.ShapeDtypeStruct((s, LANES), F32),
        ],
        scratch_shapes=[pltpu.VMEM((tm, D_MODEL), BF16)],
        compiler_params=_cparams(("arbitrary", "arbitrary")),
        name="in_proj",
    )(x, g, w_in_p, c128, s128, c64, s64, gains)


def _flash_slab(q, k_slab, v_slab, masks, carry):
    m, l, acc = carry
    sc = _dot_t(q, k_slab)
    if masks is not None:
        tk = sc.shape[1] // len(masks)
        sc = jnp.concatenate([jnp.where(mk, sc[:, u * tk:(u + 1) * tk], NEG)
                              for u, mk in enumerate(masks)], axis=1)
    m_new = jnp.maximum(m, jnp.max(sc, axis=-1, keepdims=True))
    p = jnp.exp2(sc - m_new)
    alpha = jnp.exp2(m - m_new)
    l = alpha * l + jnp.sum(p, axis=-1, keepdims=True)
    acc = alpha * acc + jnp.dot(p.astype(BF16), v_slab, preferred_element_type=F32)
    return m_new, l, acc


def _flash_init(rows):
    return (jnp.full((rows, 1), NEG, F32), jnp.zeros((rows, 1), F32),
            jnp.zeros((rows, HEAD_DIM), F32))


def _flash_loop(streams, first_slab, n_slabs, states, shared_masks=None):
    def slab(g, sts):
        common = None if shared_masks is None else shared_masks(g)
        out = []
        for (q, load_slab), st in zip(streams, sts):
            k_slab, v_slab, masks = load_slab(g)
            out.append(_flash_slab(q, k_slab, v_slab, masks if common is None else common, st))
        return tuple(out)
    return lax.fori_loop(first_slab, n_slabs, slab, tuple(states))


def _flash_finish(state):
    _, l, acc = state
    return acc / l


def _moba_kernel(q_ref, k_ref, v_ref, o_ref, kbar_scr, *, nb, n_sel):
    blk = MOBA_BLOCK
    i = pl.program_id(1)

    @pl.when(i == 0)
    def _():
        kbar_scr[...] = jnp.zeros_like(kbar_scr)
        for hh in range(MOBA_HEADS_PER_STEP):
            def mean_block(n, c, hh=hh):
                kk = k_ref[hh, pl.ds(pl.multiple_of(n * blk, blk), blk), :].astype(F32)
                kbar_scr[hh, pl.ds(n, 1), :] = jnp.sum(kk, axis=0, keepdims=True) * (1.0 / blk)
                return c
            lax.fori_loop(0, nb, mean_block, 0)

    blk_id = lax.broadcasted_iota(I32, (LANES, blk), 0)
    valid = blk_id < i
    bl = lax.broadcasted_iota(I32, (blk, LANES), 1)

    def stream(hh):
        q = q_ref[hh]
        gate = _dot_t(kbar_scr[hh].astype(BF16), q)
        g = jnp.where(valid, gate, -jnp.inf)
        sel_t = jnp.zeros((LANES, blk), F32)
        for _ in range(n_sel):
            mx = jnp.max(g, axis=0, keepdims=True)
            first = jnp.min(jnp.where(g == mx, blk_id, LANES), axis=0, keepdims=True)
            hit = blk_id == first
            sel_t = jnp.where(hit, 1.0, sel_t)
            g = jnp.where(hit, -jnp.inf, g)
        sel = jnp.where(valid, sel_t, 0.0).T

        lim_tab = jnp.where(bl == i, row, jnp.where(sel > 0.0, blk - 1.0, -1.0))

        def load(g):
            masks = []
            for u in range(FLASH_WIDTH):
                j = g * FLASH_WIDTH + u
                lim = jnp.max(jnp.where(bl == j, lim_tab, -1.0), axis=-1, keepdims=True)
                masks.append(cols <= lim)
            sl = pl.ds(pl.multiple_of(g * slab_keys, slab_keys), slab_keys)
            return k_ref[hh, sl, :], v_ref[hh, sl, :], masks
        return q, load

    row = lax.broadcasted_iota(I32, (blk, LANES), 0).astype(F32)
    cols = lax.broadcasted_iota(I32, (blk, blk), 1).astype(F32)
    slab_keys = FLASH_WIDTH * blk
    streams = [stream(hh) for hh in range(MOBA_HEADS_PER_STEP)]
    n_slabs = (i + FLASH_WIDTH) // FLASH_WIDTH
    states = _flash_loop(streams, 0, n_slabs, [_flash_init(blk) for _ in streams])
    for hh, st in enumerate(states):
        o_ref[hh] = _flash_finish(st).astype(BF16)


def _moba(main, s):
    blk = MOBA_BLOCK
    nb = s // blk
    hps = MOBA_HEADS_PER_STEP
    assert nb <= LANES and nb % FLASH_WIDTH == 0 and MOBA_HEADS % hps == 0
    n_sel = max(1, min(MOBA_TOPK, nb - 1))
    return pl.pallas_call(
        functools.partial(_moba_kernel, nb=nb, n_sel=n_sel),
        grid=(MOBA_HEADS // hps, nb),
        in_specs=[
            pl.BlockSpec((hps, blk, LANES), lambda h, i: (BLK_MQ // hps + h, i, 0)),
            pl.BlockSpec((hps, s, LANES), lambda h, i: (BLK_MK // hps + h, 0, 0)),
            pl.BlockSpec((hps, s, LANES), lambda h, i: (BLK_MV // hps + h, 0, 0)),
        ],
        out_specs=pl.BlockSpec((hps, blk, LANES), lambda h, i: (h, i, 0)),
        out_shape=jax.ShapeDtypeStruct((MOBA_HEADS, s, LANES), BF16),
        scratch_shapes=[pltpu.VMEM((hps, LANES, LANES), F32)],
        compiler_params=_cparams(("arbitrary", "arbitrary")),
        name="moba",
    )(main, main, main)


def _diff_kernel(q_ref, k_ref, v_ref, lam_ref, g_ref, o_ref, *, tq, lam_init):
    i = pl.program_id(1)
    lane = lax.broadcasted_iota(I32, (tq, LANES), 1)
    slab_keys = FLASH_WIDTH * tq
    row = lax.broadcasted_iota(I32, (2 * tq, 1), 0) & (tq - 1)
    cols = lax.broadcasted_iota(I32, (2 * tq, tq), 1)

    def tail_masks(g):
        masks = []
        for u in range(FLASH_WIDTH):
            j = g * FLASH_WIDTH + u
            masks.append(cols <= jnp.where(j < i, tq - 1, jnp.where(j == i, row, -1)))
        return masks

    def stream(hh):
        q = q_ref[hh]
        zero = jnp.zeros_like(q)
        qq = jnp.concatenate([jnp.where(lane < 64, q, zero), jnp.where(lane < 64, zero, q)], axis=0)

        def load(g):
            sl = pl.ds(pl.multiple_of(g * slab_keys, slab_keys), slab_keys)
            return k_ref[hh, sl, :], v_ref[hh, sl, :], None
        return qq, load

    streams = [stream(hh) for hh in range(DIFF_HEADS_PER_STEP)]
    full = i // FLASH_WIDTH
    states = _flash_loop(streams, 0, full, [_flash_init(2 * tq) for _ in streams])
    states = _flash_loop(streams, full, full + 1, states, shared_masks=tail_masks)
    lf = lam_ref[...]
    lam = (jnp.exp(jnp.sum(lf[0:1] * lf[1:2], axis=-1, keepdims=True))
           - jnp.exp(jnp.sum(lf[2:3] * lf[3:4], axis=-1, keepdims=True)) + lam_init)
    for hh, st in enumerate(states):
        o = _flash_finish(st)
        a = o[:tq] - lam * o[tq:]
        o_ref[hh] = (_rms_rows(a, g_ref[...]) * (1.0 - lam_init)).astype(BF16)


def _diff(main, lam_params, subln, layer, s):
    tq = min(s, 256)
    hps = DIFF_HEADS_PER_STEP
    assert (s // tq) % FLASH_WIDTH == 0 and DIFF_HEADS % hps == 0
    lam_init = 0.8 - 0.6 * math.exp(-0.3 * layer)
    return pl.pallas_call(
        functools.partial(_diff_kernel, tq=tq, lam_init=lam_init),
        grid=(DIFF_HEADS // hps, s // tq),
        in_specs=[
            pl.BlockSpec((hps, tq, LANES), lambda h, i: (BLK_DQ // hps + h, i, 0)),
            pl.BlockSpec((hps, s, LANES), lambda h, i: (BLK_DK // hps + h, 0, 0)),
            pl.BlockSpec((hps, s, LANES), lambda h, i: (BLK_DV // hps + h, 0, 0)),
            pl.BlockSpec((4, DIFF_QK_DIM), lambda h, i: (0, 0)),
            pl.BlockSpec((1, LANES), lambda h, i: (0, 0)),
        ],
        out_specs=pl.BlockSpec((hps, tq, LANES), lambda h, i: (h, i, 0)),
        out_shape=jax.ShapeDtypeStruct((DIFF_HEADS, s, LANES), BF16),
        compiler_params=_cparams(("arbitrary", "arbitrary")),
        name="diff_attn",
    )(main, main, main, lam_params, subln)


DSA_TQ = 256


def _sort_key(sc):
    sc = jnp.where(sc == 0.0, 0.0, sc)
    b = lax.bitcast_convert_type(sc, I32)
    return b ^ ((b >> 31) & 0x7FFFFFFF)


def _dsa_kernel(qi_ref, ki_ref, w_ref, q_ref, k_ref, v_ref, o_ref,
                key_scr, keyt_scr, wb_scr, thr_scr, *, n_keep):
    t = DSA_TQ
    i = pl.program_id(0)
    w = w_ref[...]
    for h in range(IDX_HEADS):
        wb_scr[h] = jnp.broadcast_to(w[:, 64 + h:65 + h], (t, t))

    def tile(j):
        return pl.ds(pl.multiple_of(j * t, t), t)

    q_idx = qi_ref[...].reshape(IDX_HEADS // 2 * t, LANES)

    def index_scores(j):
        s_even = _dot_t(q_idx, ki_ref[0, tile(j), :])
        s_odd = _dot_t(q_idx, ki_ref[1, tile(j), :])
        sc = jnp.zeros((t, t), F32)
        for b in range(IDX_HEADS // 2):
            rs = slice(b * t, (b + 1) * t)
            sc = sc + wb_scr[2 * b] * jnp.maximum(s_even[rs], 0.0)
            sc = sc + wb_scr[2 * b + 1] * jnp.maximum(s_odd[rs], 0.0)
        return sc

    rows = lax.broadcasted_iota(I32, (t, t), 0)
    cols = lax.broadcasted_iota(I32, (t, t), 1)
    causal = cols <= rows

    def put_keys(j, keys):
        key_scr[:, tile(j)] = keys
        keyt_scr[tile(j), :] = keys.T

    def fill_past(j, c):
        put_keys(j, _sort_key(index_scores(j)))
        return c
    lax.fori_loop(0, i, fill_past, 0)
    put_keys(i, jnp.where(causal, _sort_key(index_scores(i)), INT_MIN))
    n_steps = (i + FLASH_WIDTH) // FLASH_WIDTH

    def fill_pad(j, c):
        key_scr[:, tile(j)] = jnp.full((t, t), INT_MIN, I32)
        return c
    lax.fori_loop(i + 1, n_steps * FLASH_WIDTH, fill_pad, 0)

    def count_keys(preds):
        def step(j, cnts):
            kk = keyt_scr[tile(j), :].reshape(t // 8, 8, t)
            return tuple(c + jnp.sum(jnp.where(p(kk), 1, 0), axis=0) for c, p in zip(cnts, preds))

        def pair(jj, cnts):
            return step(2 * jj + 1, step(2 * jj, cnts))
        zero = jnp.zeros((8, t), I32)
        n_pairs = (i + 1) // 2
        cnts = lax.fori_loop(0, n_pairs, pair, tuple(zero for _ in preds))
        cnts = lax.fori_loop(2 * n_pairs, i + 1, step, cnts)
        return [jnp.sum(c, axis=0, keepdims=True) for c in cnts]

    def bit_body(b, prefix):
        cand_u = prefix | lax.shift_left(jnp.int32(1), 31 - b)
        cand = cand_u ^ INT_MIN
        total, = count_keys([lambda kk: kk >= cand])
        return jnp.where(total >= n_keep, cand_u, prefix)

    prefix = lax.fori_loop(0, 32, bit_body, jnp.zeros((8, t), I32))
    thr_t = jnp.maximum(prefix ^ INT_MIN, INT_MIN + 1)
    thr_scr[...] = jnp.broadcast_to(thr_t[0:1], (t, t)).T

    n_gt_t, n_ge_t = count_keys([lambda kk: kk > thr_t, lambda kk: kk >= thr_t])
    surplus = jnp.max(jnp.where(n_ge_t > n_keep, 1, 0))

    @pl.when(surplus > 0)
    def _():
        need_t = (n_keep - n_gt_t).astype(F32)
        need = jnp.broadcast_to(need_t, (t, t)).T[:, 0:1]
        thr = thr_scr[...]
        upper = jnp.where(rows <= cols, 1.0, 0.0).astype(BF16)

        def drop_late_ties(j, before):
            kk = key_scr[:, tile(j)]
            tie = kk == thr
            seen = before + jnp.dot(jnp.where(tie, 1.0, 0.0).astype(BF16), upper,
                                    preferred_element_type=F32)
            key_scr[:, tile(j)] = jnp.where(tie, jnp.where(seen > need, INT_MIN, kk), kk)
            return seen[:, t - 1:t]
        lax.fori_loop(0, i + 1, drop_late_ties, jnp.zeros((t, 1), F32))

    slab_keys = FLASH_WIDTH * t

    def admitted(g):
        return [key_scr[:, tile(g * FLASH_WIDTH + u)] >= thr_scr[...] for u in range(FLASH_WIDTH)]

    def head_group(hg, c):
        def stream(h):
            def load(g):
                sl = pl.ds(pl.multiple_of(g * slab_keys, slab_keys), slab_keys)
                return k_ref[h, sl, :], v_ref[h, sl, :], None
            return q_ref[h], load

        heads = [hg * DSA_HEADS_PER_STEP + u for u in range(DSA_HEADS_PER_STEP)]
        states = _flash_loop([stream(h) for h in heads], 0, n_steps, [_flash_init(t) for _ in heads],
                             shared_masks=admitted)
        for h, st in zip(heads, states):
            o_ref[h] = _flash_finish(st).astype(BF16)
        return c
    lax.fori_loop(0, DSA_HEADS // DSA_HEADS_PER_STEP, head_group, 0)


def _dsa(main, tail, s):
    t = DSA_TQ
    n_keep = min(DSA_TOPK, s // 4)
    slab = FLASH_WIDTH * t
    s_pad = -(-s // slab) * slab
    resident = dict(pipeline_mode=pl.Buffered(1))
    return pl.pallas_call(
        functools.partial(_dsa_kernel, n_keep=n_keep),
        grid=(s // t,),
        in_specs=[
            pl.BlockSpec((IDX_HEADS // 2, t, LANES), lambda i: (BLK_IQ // 4, i, 0)),
            pl.BlockSpec((2, s, LANES), lambda i: (BLK_IK // 2, 0, 0), **resident),
            pl.BlockSpec((t, LANES), lambda i: (i, 0)),
            pl.BlockSpec((DSA_HEADS, t, LANES), lambda i: (BLK_SQ // 6, i, 0)),
            pl.BlockSpec((DSA_HEADS, s, LANES), lambda i: (BLK_SK // 6, 0, 0), **resident),
            pl.BlockSpec((DSA_HEADS, s, LANES), lambda i: (BLK_SV // 6, 0, 0), **resident),
        ],
        out_specs=pl.BlockSpec((DSA_HEADS, t, LANES), lambda i: (0, i, 0)),
        out_shape=jax.ShapeDtypeStruct((DSA_HEADS, s, LANES), BF16),
        scratch_shapes=[
            pltpu.VMEM((t, s_pad), I32),
            pltpu.VMEM((s, t), I32),
            pltpu.VMEM((IDX_HEADS, t, t), F32),
            pltpu.VMEM((t, t), I32),
        ],
        compiler_params=_cparams(("arbitrary",)),
        name="dsa",
    )(main, main, tail, main, main, main)


def _memkv_kernel(mem_ref, g_ref, wkv_ref, gk_ref, ck_ref, cv_ref):
    m = _rms_rows(mem_ref[...], g_ref[...]).astype(BF16)
    kv = jnp.dot(m, wkv_ref[...], preferred_element_type=F32)
    for h in range(CROSS_HEADS):
        sl = slice(h * HEAD_DIM, (h + 1) * HEAD_DIM)
        ck_ref[:, sl] = _rms_rows(kv[:, sl], gk_ref[...]).astype(BF16)
    cv_ref[...] = kv[:, CROSS_W:].astype(BF16)


def _memkv(mem2d, g, wkv, gk):
    n = mem2d.shape[0]
    out = jax.ShapeDtypeStruct((n, CROSS_W), BF16)
    return pl.pallas_call(
        _memkv_kernel,
        out_shape=[out, out],
        compiler_params=pltpu.CompilerParams(vmem_limit_bytes=VMEM_LIMIT),
        name="mem_kv",
    )(mem2d, g, wkv, gk)


def _mid_kernel(x_ref, om_ref, od_ref, os_ref, wout_ref, gc_ref, wq_ref, gq_ref,
                ck_ref, cv_ref, wo_ref, o_ref):
    heads = ([om_ref[h] for h in range(MOBA_HEADS)] + [od_ref[h] for h in range(DIFF_HEADS)]
             + [os_ref[h] for h in range(DSA_HEADS)])
    mixed = jnp.concatenate(heads, axis=-1)
    x1 = x_ref[...] + jnp.dot(mixed, wout_ref[...], preferred_element_type=F32)
    hq = _rms_rows(x1, gc_ref[...]).astype(BF16)
    cq = jnp.dot(hq, wq_ref[...], preferred_element_type=F32)
    scale = HEAD_DIM ** -0.5
    outs = []
    for h in range(CROSS_HEADS):
        sl = slice(h * HEAD_DIM, (h + 1) * HEAD_DIM)
        qh = _rms_rows(cq[:, sl], gq_ref[...]).astype(BF16)
        sc = _dot_t(qh, ck_ref[:, sl]) * scale
        m = jnp.max(sc, axis=-1, keepdims=True)
        p = jnp.exp(sc - m)
        l = jnp.sum(p, axis=-1, keepdims=True)
        outs.append(jnp.dot(p.astype(BF16), cv_ref[:, sl], preferred_element_type=F32) / l)
    co = jnp.concatenate(outs, axis=-1).astype(BF16)
    o_ref[...] = x1 + jnp.dot(co, wo_ref[...], preferred_element_type=F32)


def _mid(x, om, od, osa, wout, gc, wq, gq, ck, cv, wo, s):
    tm = min(s, 512)
    n_mem = ck.shape[0]
    whole = lambda shape: pl.BlockSpec(shape, lambda i: (0,) * len(shape), pipeline_mode=pl.Buffered(1))
    heads = lambda n: pl.BlockSpec((n, tm, LANES), lambda i: (0, i, 0))
    return pl.pallas_call(
        _mid_kernel,
        grid=(s // tm,),
        in_specs=[
            pl.BlockSpec((tm, D_MODEL), lambda i: (i, 0)),
            heads(MOBA_HEADS), heads(DIFF_HEADS), heads(DSA_HEADS),
            whole((D_MODEL, D_MODEL)), whole((1, D_MODEL)), whole((D_MODEL, CROSS_W)),
            whole((1, LANES)), whole((n_mem, CROSS_W)), whole((n_mem, CROSS_W)),
            whole((CROSS_W, D_MODEL)),
        ],
        out_specs=pl.BlockSpec((tm, D_MODEL), lambda i: (i, 0)),
        out_shape=jax.ShapeDtypeStruct((s, D_MODEL), F32),
        compiler_params=_cparams(("arbitrary",)),
        name="out_proj_cross",
    )(x, om, od, osa, wout, gc, wq, gq, ck, cv, wo)


FFN_HALO = 16
FFN_ROW_CHUNK = 256

def _ffn_kernel(x_ref, xp_ref, g_ref, wg_ref, wv_ref, cwg_ref, cwv_ref, cbg_ref, cbv_ref,
                wd_ref, o_ref, h_scr, *, tm):
    i = pl.program_id(0)
    f = pl.program_id(1)

    @pl.when(f == 0)
    def _():
        prev = _rms_rows(xp_ref[...], g_ref[...])
        h_scr[0:FFN_HALO, :] = jnp.where(i > 0, prev, 0.0).astype(BF16)
        h_scr[FFN_HALO:, :] = _rms_rows(x_ref[...], g_ref[...]).astype(BF16)
        o_ref[...] = x_ref[...]

    for r0 in range(0, tm, FFN_ROW_CHUNK):
        h = h_scr[r0:r0 + FFN_ROW_CHUNK + FFN_HALO, :]

        def conv(w_ref, cw_ref, cb_ref):
            u = jnp.dot(h, w_ref[0], preferred_element_type=F32)
            cw = cw_ref[...]
            uc = (cw[0:1] * pltpu.roll(u, 2, 0) + cw[1:2] * pltpu.roll(u, 1, 0) + cw[2:3] * u
                  + cb_ref[...])
            return uc[FFN_HALO:, :]

        gate = conv(wg_ref, cwg_ref, cbg_ref)
        val = conv(wv_ref, cwv_ref, cbv_ref)
        act = (gate * jax.nn.sigmoid(gate) * val).astype(BF16)
        o_ref[r0:r0 + FFN_ROW_CHUNK, :] += jnp.dot(act, wd_ref[...], preferred_element_type=F32)


def _ffn(x, g, w_up_p, cw_p, cb_p, w_down_p, s):
    tm = min(s, 512)
    nf = D_FF_PAD // FF_TILE
    halo_blocks = tm // FFN_HALO
    return pl.pallas_call(
        functools.partial(_ffn_kernel, tm=tm),
        grid=(s // tm, nf),
        in_specs=[
            pl.BlockSpec((tm, D_MODEL), lambda i, f: (i, 0)),
            pl.BlockSpec((FFN_HALO, D_MODEL), lambda i, f: (jnp.maximum(i * halo_blocks - 1, 0), 0)),
            pl.BlockSpec((1, D_MODEL), lambda i, f: (0, 0)),
            pl.BlockSpec((1, D_MODEL, FF_TILE), lambda i, f: (f, 0, 0)),
            pl.BlockSpec((1, D_MODEL, FF_TILE), lambda i, f: (f + nf, 0, 0)),
            pl.BlockSpec((3, FF_TILE), lambda i, f: (0, f)),
            pl.BlockSpec((3, FF_TILE), lambda i, f: (0, f + nf)),
            pl.BlockSpec((1, FF_TILE), lambda i, f: (0, f)),
            pl.BlockSpec((1, FF_TILE), lambda i, f: (0, f + nf)),
            pl.BlockSpec((FF_TILE, D_MODEL), lambda i, f: (f, 0)),
        ],
        out_specs=pl.BlockSpec((tm, D_MODEL), lambda i, f: (i, 0)),
        out_shape=jax.ShapeDtypeStruct((s, D_MODEL), F32),
        scratch_shapes=[pltpu.VMEM((tm + FFN_HALO, D_MODEL), BF16)],
        compiler_params=_cparams(("arbitrary", "arbitrary")),
        name="ffn",
    )(x, x, g, w_up_p, w_up_p, cw_p, cw_p, cb_p, cb_p, w_down_p)


def _pad_cols(a, width):
    return jnp.pad(a, ((0, 0), (0, width - a.shape[1])))


def _col_tiles(w, width):
    k, n = w.shape
    return w.reshape(k, n // width, width).transpose(1, 0, 2)


def _split_pad(a):
    r = a.shape[0]
    halves = jnp.pad(a.reshape(r, 2, D_FF), ((0, 0), (0, 0), (0, D_FF_PAD - D_FF)))
    return halves.reshape(r, 2 * D_FF_PAD)


def kernel(x, mem, positions, attn_norm, w_in, moba_qk_gain, diff_qk_gain, diff_lambda,
           diff_subln, dsa_qk_gain, w_out, cross_norm, mem_norm, cross_wq, cross_wkv,
           cross_qk_gain, cross_wo, ffn_norm, ffn_w_up, ffn_conv_w, ffn_conv_b, ffn_w_down):
    b, s, _ = x.shape
    assert b == 1 and s % DSA_TQ == 0 and s % MOBA_BLOCK == 0
    xs = x.reshape(s, D_MODEL)
    mem2d = mem.reshape(mem.shape[1], D_MODEL)
    tabs = _rope_tables(positions, s)
    row = lambda v: v.reshape(1, -1)
    for l in range(DEPTH):
        gains = jnp.concatenate([
            moba_qk_gain[l], dsa_qk_gain[l], jnp.tile(diff_qk_gain[l], (1, 2)),
            jnp.zeros((2, LANES), F32)], axis=0)
        w_in_p = _col_tiles(_pad_cols(w_in[l], D_IN_PAD).astype(BF16), PROJ_TN)
        main, tail = _in_proj(xs, row(attn_norm[l]), w_in_p, tabs, gains, s)
        o_moba = _moba(main, s)
        o_diff = _diff(main, diff_lambda[l], row(diff_subln[l]), l, s)
        o_dsa = _dsa(main, tail, s)
        ck, cv = _memkv(mem2d, row(mem_norm[l]), cross_wkv[l].astype(BF16), row(cross_qk_gain[l, 1]))
        xs = _mid(xs, o_moba, o_diff, o_dsa, w_out[l].astype(BF16), row(cross_norm[l]),
                  cross_wq[l].astype(BF16), row(cross_qk_gain[l, 0]), ck, cv,
                  cross_wo[l].astype(BF16), s)
        w_up_p = _col_tiles(_split_pad(ffn_w_up[l]).astype(BF16), FF_TILE)
        cw_p = _split_pad(ffn_conv_w[l])
        cb_p = _split_pad(row(ffn_conv_b[l]))
        w_down_p = jnp.pad(ffn_w_down[l], ((0, D_FF_PAD - D_FF), (0, 0))).astype(BF16)
        xs = _ffn(xs, row(ffn_norm[l]), w_up_p, cw_p, cb_p, w_down_p, s)
    return xs.reshape(b, s, D_MODEL)
```

```python
import functools
import math

import jax
import jax.numpy as jnp
from jax import lax
from jax.experimental import pallas as pl
from jax.experimental.pallas import tpu as pltpu

F32 = jnp.float32
BF16 = jnp.bfloat16
I32 = jnp.int32

D_MODEL = 2048
DEPTH = 2
HEAD_DIM = 128
MOBA_HEADS = 6
DIFF_HEADS = 4
DSA_HEADS = 6
MOBA_BLOCK = 256
MOBA_TOPK = 3
DIFF_QK_DIM = 64
DSA_TOPK = 256
IDX_HEADS = 8
IDX_DIM = 64
MEM_LEN = 256
CROSS_HEADS = 4
CROSS_W = CROSS_HEADS * HEAD_DIM
D_FF = 5504
ROPE_THETA = 10000.0
RMS_EPS = 1e-6

LANES = 128
PROJ_TN = 256
PROJ_ROW_CHUNK = 128
D_IN = 6728
D_IN_PAD = 6912
D_IN_TAIL = D_IN - (D_IN_PAD - 256)
N_PROJ_TILES = D_IN_PAD // PROJ_TN
N_HEAD_BLOCKS = D_IN_PAD // LANES
FF_TILE = 512
D_FF_PAD = 5632
NEG = -1e30
LOG2E = 1.4426950408889634
FLASH_WIDTH = 4
MOBA_HEADS_PER_STEP = 3
DSA_HEADS_PER_STEP = 3
DIFF_HEADS_PER_STEP = 2
INT_MIN = -(2 ** 31)
VMEM_LIMIT = 56 * 1024 * 1024

BLK_MQ, BLK_MK, BLK_MV = 0, 6, 12
BLK_DQ, BLK_DK, BLK_DV = 18, 22, 26
BLK_SQ, BLK_SK, BLK_SV = 30, 36, 42
BLK_IQ, BLK_IK = 48, 52


def _cparams(sem):
    return pltpu.CompilerParams(dimension_semantics=sem, vmem_limit_bytes=VMEM_LIMIT)


def _rms_rows(x, g):
    return x * lax.rsqrt(jnp.mean(x * x, axis=-1, keepdims=True) + RMS_EPS) * g


def _dot_t(a, b):
    return lax.dot_general(a, b, (((1,), (1,)), ((), ())), preferred_element_type=F32)


def _rope_tab_kernel(pos_ref, inv128_ref, inv64_ref, c128_ref, s128_ref, c64_ref, s64_ref):
    p = pos_ref[...].astype(F32)
    lane = lax.broadcasted_iota(I32, (1, LANES), 1)
    a = p * inv128_ref[...]
    sa = jnp.sin(a)
    c128_ref[...] = jnp.cos(a)
    s128_ref[...] = jnp.where(lane < 64, -sa, sa)
    b = p * inv64_ref[...]
    sb = jnp.sin(b)
    c64_ref[...] = jnp.cos(b)
    s64_ref[...] = jnp.where((lane % 64) < 32, -sb, sb)


def _rope_tables(positions, s):
    tm = min(s, 1024)
    inv_a = ROPE_THETA ** (-(jnp.arange(64, dtype=F32) * 2.0 / 128))
    inv_b = ROPE_THETA ** (-(jnp.arange(32, dtype=F32) * 2.0 / 64))
    inv128 = jnp.tile(inv_a, 2).reshape(1, LANES)
    inv64 = jnp.tile(inv_b, 4).reshape(1, LANES)
    pos = positions.reshape(s, 1)
    tab = jax.ShapeDtypeStruct((s, LANES), F32)
    row = pl.BlockSpec((tm, LANES), lambda i: (i, 0))
    cst = pl.BlockSpec((1, LANES), lambda i: (0, 0))
    return pl.pallas_call(
        _rope_tab_kernel,
        grid=(s // tm,),
        in_specs=[pl.BlockSpec((tm, 1), lambda i: (i, 0)), cst, cst],
        out_specs=[row, row, row, row],
        out_shape=[tab, tab, tab, tab],
        compiler_params=_cparams(("arbitrary",)),
        name="rope_tables",
    )(pos, inv128, inv64)


def _proj_kernel(x_ref, g_ref, w_ref, c128_ref, s128_ref, c64_ref, s64_ref, gains_ref,
                 main_ref, tail_ref, h_scr):
    j = pl.program_id(1)

    @pl.when(j == 0)
    def _():
        h_scr[...] = _rms_rows(x_ref[...], g_ref[...]).astype(BF16)

    lane = lax.broadcasted_iota(I32, (1, LANES), 1)
    lo = lane < 64
    tm = h_scr.shape[0]
    chunks = [slice(r0, r0 + PROJ_ROW_CHUNK) for r0 in range(0, tm, PROJ_ROW_CHUNK)]

    def product(rs, w):
        return jnp.dot(h_scr[rs, :], w, preferred_element_type=F32)

    def rope128(y, rs):
        return y * c128_ref[rs, :] + pltpu.roll(y, 64, 1) * s128_ref[rs, :]

    def rope64(y, rs):
        partner = jnp.where((lane % 64) < 32, pltpu.roll(y, 96, 1), pltpu.roll(y, 32, 1))
        return y * c64_ref[rs, :] + partner * s64_ref[rs, :]

    def norm64(y, g):
        y2 = y * y
        s_lo = jnp.sum(jnp.where(lo, y2, 0.0), axis=-1, keepdims=True)
        s_hi = jnp.sum(jnp.where(lo, 0.0, y2), axis=-1, keepdims=True)
        ms = jnp.where(lo, s_lo, s_hi) * (1.0 / 64)
        return y * lax.rsqrt(ms + RMS_EPS) * g

    def emit(fn):
        w = w_ref[0].astype(BF16)
        for rs in chunks:
            r = product(rs, w)
            for c in range(2):
                main_ref[c, rs, :] = fn(r[:, c * LANES:(c + 1) * LANES], rs).astype(BF16)

    def qk128(row, post=None):
        fn = lambda y, rs: rope128(_rms_rows(y, gains_ref[row:row + 1, :]), rs)
        return fn if post is None else (lambda y, rs: fn(y, rs) * post)

    def qk64(row, post=None):
        fn = lambda y, rs: rope64(norm64(y, gains_ref[row:row + 1, :]), rs)
        return fn if post is None else (lambda y, rs: fn(y, rs) * post)

    ident = lambda y, rs: y
    q128 = HEAD_DIM ** -0.5 * LOG2E
    q64 = DIFF_QK_DIM ** -0.5 * LOG2E
    segments = (
        (0, 3, qk128(0, q128)), (3, 6, qk128(1)), (6, 9, ident),
        (9, 11, qk64(4, q64)), (11, 13, qk64(5)), (13, 15, ident),
        (15, 18, qk128(2, q128)), (18, 21, qk128(3)), (21, 24, ident),
        (24, 26, rope64),
    )
    for first, last, fn in segments:
        pl.when((j >= first) & (j < last))(functools.partial(emit, fn))

    @pl.when(j == N_PROJ_TILES - 1)
    def _():
        w = w_ref[0, :, 0:LANES].astype(BF16)
        for rs in chunks:
            y = jnp.where(lane < D_IN_TAIL, product(rs, w), 0.0)
            kk = jnp.where(lo, rope64(y, rs), 0.0)
            main_ref[0, rs, :] = kk.astype(BF16)
            main_ref[1, rs, :] = pltpu.roll(kk, 64, 1).astype(BF16)
            tail_ref[rs, :] = (y * (IDX_HEADS ** -0.5)) * (IDX_DIM ** -0.5)


def _in_proj(x, g, w_in, layer, tabs, gains, s):
    tm = min(s, 1024)
    c128, s128, c64, s64 = tabs
    row = pl.BlockSpec((tm, LANES), lambda i, j: (i, 0))
    return pl.pallas_call(
        _proj_kernel,
        grid=(s // tm, N_PROJ_TILES),
        in_specs=[
            pl.BlockSpec((tm, D_MODEL), lambda i, j: (i, 0)),
            pl.BlockSpec((1, D_MODEL), lambda i, j: (0, 0)),
            pl.BlockSpec((1, D_MODEL, PROJ_TN), lambda i, j: (layer, 0, j)),
            row, row, row, row,
            pl.BlockSpec((8, LANES), lambda i, j: (0, 0)),
        ],
        out_specs=[
            pl.BlockSpec((2, tm, LANES), lambda i, j: (j, i, 0)),
            pl.BlockSpec((tm, LANES), lambda i, j: (i, 0)),
        ],
        out_shape=[
            jax.ShapeDtypeStruct((N_HEAD_BLOCKS, s, LANES), BF16),
            jax.ShapeDtypeStruct((s, LANES), F32),
        ],
        scratch_shapes=[pltpu.VMEM((tm, D_MODEL), BF16)],
        compiler_params=_cparams(("arbitrary", "arbitrary")),
        name="in_proj",
    )(x, g, w_in, c128, s128, c64, s64, gains)


def _flash_slab(q, k_slab, v_slab, masks, carry, bias=None):
    m, l, acc = carry
    sc = _dot_t(q, k_slab)
    if bias is not None:
        sc = sc + bias
    if masks is not None:
        tk = sc.shape[1] // len(masks)
        sc = jnp.concatenate([jnp.where(mk, sc[:, u * tk:(u + 1) * tk], NEG)
                              for u, mk in enumerate(masks)], axis=1)
    m_new = jnp.maximum(m, jnp.max(sc, axis=-1, keepdims=True))
    p = jnp.exp2(sc - m_new)
    alpha = jnp.exp2(m - m_new)
    l = alpha * l + jnp.sum(p, axis=-1, keepdims=True)
    acc = alpha * acc + jnp.dot(p.astype(BF16), v_slab, preferred_element_type=F32)
    return m_new, l, acc


def _flash_init(rows):
    return (jnp.full((rows, 1), NEG, F32), jnp.zeros((rows, 1), F32),
            jnp.zeros((rows, HEAD_DIM), F32))


def _flash_loop(streams, first_slab, n_slabs, states, shared_masks=None, shared_bias=None):
    def slab(g, sts):
        common = None if shared_masks is None else shared_masks(g)
        bias = None if shared_bias is None else shared_bias(g)
        out = []
        for (q, load_slab), st in zip(streams, sts):
            k_slab, v_slab, masks = load_slab(g)
            out.append(_flash_slab(q, k_slab, v_slab, masks if common is None else common, st, bias))
        return tuple(out)
    return lax.fori_loop(first_slab, n_slabs, slab, tuple(states))


def _flash_finish(state):
    _, l, acc = state
    return acc / l


def _moba_kernel(q_ref, k_ref, v_ref, o_ref, kbar_scr, *, nb, n_sel):
    blk = MOBA_BLOCK
    i = pl.program_id(1)

    @pl.when(i == 0)
    def _():
        kbar_scr[...] = jnp.zeros_like(kbar_scr)
        for hh in range(MOBA_HEADS_PER_STEP):
            def mean_block(n, c, hh=hh):
                kk = k_ref[hh, pl.ds(pl.multiple_of(n * blk, blk), blk), :].astype(F32)
                kbar_scr[hh, pl.ds(n, 1), :] = jnp.sum(kk, axis=0, keepdims=True) * (1.0 / blk)
                return c
            lax.fori_loop(0, nb, mean_block, 0)

    blk_id = lax.broadcasted_iota(I32, (LANES, blk), 0)
    valid = blk_id < i
    bl = lax.broadcasted_iota(I32, (blk, LANES), 1)

    def stream(hh):
        q = q_ref[hh]
        gate = _dot_t(kbar_scr[hh].astype(BF16), q)
        g = jnp.where(valid, gate, -jnp.inf)
        sel_t = jnp.zeros((LANES, blk), F32)
        for _ in range(n_sel):
            mx = jnp.max(g, axis=0, keepdims=True)
            first = jnp.min(jnp.where(g == mx, blk_id, LANES), axis=0, keepdims=True)
            hit = blk_id == first
            sel_t = jnp.where(hit, 1.0, sel_t)
            g = jnp.where(hit, -jnp.inf, g)
        sel = jnp.where(valid, sel_t, 0.0).T

        lim_tab = jnp.where(bl == i, row, jnp.where(sel > 0.0, blk - 1.0, -1.0))

        def load(g):
            masks = []
            for u in range(FLASH_WIDTH):
                j = g * FLASH_WIDTH + u
                lim = jnp.max(jnp.where(bl == j, lim_tab, -1.0), axis=-1, keepdims=True)
                masks.append(cols <= lim)
            sl = pl.ds(pl.multiple_of(g * slab_keys, slab_keys), slab_keys)
            return k_ref[hh, sl, :], v_ref[hh, sl, :], masks
        return q, load

    row = lax.broadcasted_iota(I32, (blk, LANES), 0).astype(F32)
    cols = lax.broadcasted_iota(I32, (blk, blk), 1).astype(F32)
    slab_keys = FLASH_WIDTH * blk
    streams = [stream(hh) for hh in range(MOBA_HEADS_PER_STEP)]
    n_slabs = (i + FLASH_WIDTH) // FLASH_WIDTH
    states = _flash_loop(streams, 0, n_slabs, [_flash_init(blk) for _ in streams])
    for hh, st in enumerate(states):
        o_ref[hh] = _flash_finish(st).astype(BF16)


def _moba(main, s):
    blk = MOBA_BLOCK
    nb = s // blk
    hps = MOBA_HEADS_PER_STEP
    assert nb <= LANES and nb % FLASH_WIDTH == 0 and MOBA_HEADS % hps == 0
    n_sel = max(1, min(MOBA_TOPK, nb - 1))
    return pl.pallas_call(
        functools.partial(_moba_kernel, nb=nb, n_sel=n_sel),
        grid=(MOBA_HEADS // hps, nb),
        in_specs=[
            pl.BlockSpec((hps, blk, LANES), lambda h, i: (BLK_MQ // hps + h, i, 0)),
            pl.BlockSpec((hps, s, LANES), lambda h, i: (BLK_MK // hps + h, 0, 0)),
            pl.BlockSpec((hps, s, LANES), lambda h, i: (BLK_MV // hps + h, 0, 0)),
        ],
        out_specs=pl.BlockSpec((hps, blk, LANES), lambda h, i: (h, i, 0)),
        out_shape=jax.ShapeDtypeStruct((MOBA_HEADS, s, LANES), BF16),
        scratch_shapes=[pltpu.VMEM((hps, LANES, LANES), F32)],
        compiler_params=_cparams(("arbitrary", "arbitrary")),
        name="moba",
    )(main, main, main)


def _diff_kernel(q_ref, k_ref, v_ref, lam_ref, g_ref, o_ref, *, tq, lam_init):
    i = pl.program_id(1)
    lane = lax.broadcasted_iota(I32, (tq, LANES), 1)
    slab_keys = FLASH_WIDTH * tq
    row = lax.broadcasted_iota(I32, (2 * tq, 1), 0) & (tq - 1)
    cols = lax.broadcasted_iota(I32, (2 * tq, tq), 1)

    def tail_masks(g):
        masks = []
        for u in range(FLASH_WIDTH):
            j = g * FLASH_WIDTH + u
            masks.append(cols <= jnp.where(j < i, tq - 1, jnp.where(j == i, row, -1)))
        return masks

    def stream(hh):
        q = q_ref[hh]
        zero = jnp.zeros_like(q)
        qq = jnp.concatenate([jnp.where(lane < 64, q, zero), jnp.where(lane < 64, zero, q)], axis=0)

        def load(g):
            sl = pl.ds(pl.multiple_of(g * slab_keys, slab_keys), slab_keys)
            return k_ref[hh, sl, :], v_ref[hh, sl, :], None
        return qq, load

    streams = [stream(hh) for hh in range(DIFF_HEADS_PER_STEP)]
    full = i // FLASH_WIDTH
    states = _flash_loop(streams, 0, full, [_flash_init(2 * tq) for _ in streams])
    states = _flash_loop(streams, full, full + 1, states, shared_masks=tail_masks)
    lf = lam_ref[...]
    lam = (jnp.exp(jnp.sum(lf[0:1] * lf[1:2], axis=-1, keepdims=True))
           - jnp.exp(jnp.sum(lf[2:3] * lf[3:4], axis=-1, keepdims=True)) + lam_init)
    for hh, st in enumerate(states):
        o = _flash_finish(st)
        a = o[:tq] - lam * o[tq:]
        o_ref[hh] = (_rms_rows(a, g_ref[...]) * (1.0 - lam_init)).astype(BF16)


def _diff(main, lam_params, subln, layer, s):
    tq = min(s, 256)
    hps = DIFF_HEADS_PER_STEP
    assert (s // tq) % FLASH_WIDTH == 0 and DIFF_HEADS % hps == 0
    lam_init = 0.8 - 0.6 * math.exp(-0.3 * layer)
    return pl.pallas_call(
        functools.partial(_diff_kernel, tq=tq, lam_init=lam_init),
        grid=(DIFF_HEADS // hps, s // tq),
        in_specs=[
            pl.BlockSpec((hps, tq, LANES), lambda h, i: (BLK_DQ // hps + h, i, 0)),
            pl.BlockSpec((hps, s, LANES), lambda h, i: (BLK_DK // hps + h, 0, 0)),
            pl.BlockSpec((hps, s, LANES), lambda h, i: (BLK_DV // hps + h, 0, 0)),
            pl.BlockSpec((4, DIFF_QK_DIM), lambda h, i: (0, 0)),
            pl.BlockSpec((1, LANES), lambda h, i: (0, 0)),
        ],
        out_specs=pl.BlockSpec((hps, tq, LANES), lambda h, i: (h, i, 0)),
        out_shape=jax.ShapeDtypeStruct((DIFF_HEADS, s, LANES), BF16),
        compiler_params=_cparams(("arbitrary", "arbitrary")),
        name="diff_attn",
    )(main, main, main, lam_params, subln)


DSA_TQ = 256


def _sort_key(sc):
    sc = jnp.where(sc == 0.0, 0.0, sc)
    b = lax.bitcast_convert_type(sc, I32)
    return b ^ ((b >> 31) & 0x7FFFFFFF)


def _dsa_kernel(qi_ref, ki_ref, w_ref, q_ref, k_ref, v_ref, o_ref,
                key_scr, keyt_scr, wb_scr, thr_scr, *, n_keep):
    t = DSA_TQ
    i = pl.program_id(0)
    w = w_ref[...]
    for h in range(IDX_HEADS):
        wb_scr[h] = jnp.broadcast_to(w[:, 64 + h:65 + h], (t, t))

    def tile(j):
        return pl.ds(pl.multiple_of(j * t, t), t)

    q_idx = qi_ref[...].reshape(IDX_HEADS // 2 * t, LANES)

    def index_scores(j):
        s_even = _dot_t(q_idx, ki_ref[0, tile(j), :])
        s_odd = _dot_t(q_idx, ki_ref[1, tile(j), :])
        sc = jnp.zeros((t, t), F32)
        for b in range(IDX_HEADS // 2):
            rs = slice(b * t, (b + 1) * t)
            sc = sc + wb_scr[2 * b] * jnp.maximum(s_even[rs], 0.0)
            sc = sc + wb_scr[2 * b + 1] * jnp.maximum(s_odd[rs], 0.0)
        return sc

    rows = lax.broadcasted_iota(I32, (t, t), 0)
    cols = lax.broadcasted_iota(I32, (t, t), 1)
    causal = cols <= rows

    def put_keys(j, keys):
        key_scr[:, tile(j)] = keys
        keyt_scr[tile(j), :] = keys.T

    def fill_past(j, c):
        put_keys(j, _sort_key(index_scores(j)))
        return c
    lax.fori_loop(0, i, fill_past, 0)
    put_keys(i, jnp.where(causal, _sort_key(index_scores(i)), INT_MIN))
    n_steps = (i + FLASH_WIDTH) // FLASH_WIDTH

    def fill_pad(j, c):
        key_scr[:, tile(j)] = jnp.full((t, t), INT_MIN, I32)
        return c
    lax.fori_loop(i + 1, n_steps * FLASH_WIDTH, fill_pad, 0)

    def count_keys(preds):
        def step(j, cnts):
            kk = keyt_scr[tile(j), :].reshape(t // 8, 8, t)
            return tuple(c + jnp.sum(jnp.where(p(kk), 1, 0), axis=0) for c, p in zip(cnts, preds))

        def pair(jj, cnts):
            return step(2 * jj + 1, step(2 * jj, cnts))
        zero = jnp.zeros((8, t), I32)
        n_pairs = (i + 1) // 2
        cnts = lax.fori_loop(0, n_pairs, pair, tuple(zero for _ in preds))
        cnts = lax.fori_loop(2 * n_pairs, i + 1, step, cnts)
        return [jnp.sum(c, axis=0, keepdims=True) for c in cnts]

    def bit_body(b, prefix):
        cand_u = prefix | lax.shift_left(jnp.int32(1), 31 - b)
        cand = cand_u ^ INT_MIN
        total, = count_keys([lambda kk: kk >= cand])
        return jnp.where(total >= n_keep, cand_u, prefix)

    prefix = lax.fori_loop(0, 32, bit_body, jnp.zeros((8, t), I32))
    thr_t = jnp.maximum(prefix ^ INT_MIN, INT_MIN + 1)
    thr_scr[...] = jnp.broadcast_to(thr_t[0:1], (t, t)).T

    n_gt_t, n_ge_t = count_keys([lambda kk: kk > thr_t, lambda kk: kk >= thr_t])
    surplus = jnp.max(jnp.where(n_ge_t > n_keep, 1, 0))

    @pl.when(surplus > 0)
    def _():
        need_t = (n_keep - n_gt_t).astype(F32)
        need = jnp.broadcast_to(need_t, (t, t)).T[:, 0:1]
        thr = thr_scr[...]
        upper = jnp.where(rows <= cols, 1.0, 0.0).astype(BF16)

        def drop_late_ties(j, before):
            kk = key_scr[:, tile(j)]
            tie = kk == thr
            seen = before + jnp.dot(jnp.where(tie, 1.0, 0.0).astype(BF16), upper,
                                    preferred_element_type=F32)
            key_scr[:, tile(j)] = jnp.where(tie, jnp.where(seen > need, INT_MIN, kk), kk)
            return seen[:, t - 1:t]
        lax.fori_loop(0, i + 1, drop_late_ties, jnp.zeros((t, 1), F32))

    slab_keys = FLASH_WIDTH * t

    def key_to_bias(j, c):
        keep = key_scr[:, tile(j)] >= thr_scr[...]
        key_scr[:, tile(j)] = lax.bitcast_convert_type(jnp.where(keep, 0.0, NEG), I32)
        return c
    lax.fori_loop(0, n_steps * FLASH_WIDTH, key_to_bias, 0)

    def admitted_bias(g):
        sl = pl.ds(pl.multiple_of(g * slab_keys, slab_keys), slab_keys)
        return lax.bitcast_convert_type(key_scr[:, sl], F32)

    def head_group(hg, c):
        def stream(h):
            def load(g):
                sl = pl.ds(pl.multiple_of(g * slab_keys, slab_keys), slab_keys)
                return k_ref[h, sl, :], v_ref[h, sl, :], None
            return q_ref[h], load

        heads = [hg * DSA_HEADS_PER_STEP + u for u in range(DSA_HEADS_PER_STEP)]
        states = _flash_loop([stream(h) for h in heads], 0, n_steps, [_flash_init(t) for _ in heads],
                             shared_bias=admitted_bias)
        for h, st in zip(heads, states):
            o_ref[h] = _flash_finish(st).astype(BF16)
        return c
    lax.fori_loop(0, DSA_HEADS // DSA_HEADS_PER_STEP, head_group, 0)


def _dsa(main, tail, s):
    t = DSA_TQ
    n_keep = min(DSA_TOPK, s // 4)
    slab = FLASH_WIDTH * t
    s_pad = -(-s // slab) * slab
    resident = dict(pipeline_mode=pl.Buffered(1))
    return pl.pallas_call(
        functools.partial(_dsa_kernel, n_keep=n_keep),
        grid=(s // t,),
        in_specs=[
            pl.BlockSpec((IDX_HEADS // 2, t, LANES), lambda i: (BLK_IQ // 4, i, 0)),
            pl.BlockSpec((2, s, LANES), lambda i: (BLK_IK // 2, 0, 0), **resident),
            pl.BlockSpec((t, LANES), lambda i: (i, 0)),
            pl.BlockSpec((DSA_HEADS, t, LANES), lambda i: (BLK_SQ // 6, i, 0)),
            pl.BlockSpec((DSA_HEADS, s, LANES), lambda i: (BLK_SK // 6, 0, 0), **resident),
            pl.BlockSpec((DSA_HEADS, s, LANES), lambda i: (BLK_SV // 6, 0, 0), **resident),
        ],
        out_specs=pl.BlockSpec((DSA_HEADS, t, LANES), lambda i: (0, i, 0)),
        out_shape=jax.ShapeDtypeStruct((DSA_HEADS, s, LANES), BF16),
        scratch_shapes=[
            pltpu.VMEM((t, s_pad), I32),
            pltpu.VMEM((s, t), I32),
            pltpu.VMEM((IDX_HEADS, t, t), F32),
            pltpu.VMEM((t, t), I32),
        ],
        compiler_params=_cparams(("arbitrary",)),
        name="dsa",
    )(main, main, tail, main, main, main)


def _memkv_kernel(mem_ref, g_ref, wkv_ref, gk_ref, ck_ref, cv_ref):
    m = _rms_rows(mem_ref[...], g_ref[...]).astype(BF16)
    kv = jnp.dot(m, wkv_ref[...], preferred_element_type=F32)
    for h in range(CROSS_HEADS):
        sl = slice(h * HEAD_DIM, (h + 1) * HEAD_DIM)
        ck_ref[:, sl] = _rms_rows(kv[:, sl], gk_ref[...]).astype(BF16)
    cv_ref[...] = kv[:, CROSS_W:].astype(BF16)


def _memkv(mem2d, g, wkv, gk):
    n = mem2d.shape[0]
    out = jax.ShapeDtypeStruct((n, CROSS_W), BF16)
    return pl.pallas_call(
        _memkv_kernel,
        out_shape=[out, out],
        compiler_params=pltpu.CompilerParams(vmem_limit_bytes=VMEM_LIMIT),
        name="mem_kv",
    )(mem2d, g, wkv, gk)


def _mid_kernel(x_ref, om_ref, od_ref, os_ref, wout_ref, gc_ref, wq_ref, gq_ref,
                ck_ref, cv_ref, wo_ref, o_ref):
    heads = ([om_ref[h] for h in range(MOBA_HEADS)] + [od_ref[h] for h in range(DIFF_HEADS)]
             + [os_ref[h] for h in range(DSA_HEADS)])
    mixed = jnp.concatenate(heads, axis=-1)
    x1 = x_ref[...] + jnp.dot(mixed, wout_ref[...], preferred_element_type=F32)
    hq = _rms_rows(x1, gc_ref[...]).astype(BF16)
    cq = jnp.dot(hq, wq_ref[...], preferred_element_type=F32)
    scale = HEAD_DIM ** -0.5
    outs = []
    for h in range(CROSS_HEADS):
        sl = slice(h * HEAD_DIM, (h + 1) * HEAD_DIM)
        qh = _rms_rows(cq[:, sl], gq_ref[...]).astype(BF16)
        sc = _dot_t(qh, ck_ref[:, sl]) * scale
        m = jnp.max(sc, axis=-1, keepdims=True)
        p = jnp.exp(sc - m)
        l = jnp.sum(p, axis=-1, keepdims=True)
        outs.append(jnp.dot(p.astype(BF16), cv_ref[:, sl], preferred_element_type=F32) / l)
    co = jnp.concatenate(outs, axis=-1).astype(BF16)
    o_ref[...] = x1 + jnp.dot(co, wo_ref[...], preferred_element_type=F32)


def _mid(x, om, od, osa, wout, gc, wq, gq, ck, cv, wo, s):
    tm = min(s, 512)
    n_mem = ck.shape[0]
    whole = lambda shape: pl.BlockSpec(shape, lambda i: (0,) * len(shape), pipeline_mode=pl.Buffered(1))
    heads = lambda n: pl.BlockSpec((n, tm, LANES), lambda i: (0, i, 0))
    return pl.pallas_call(
        _mid_kernel,
        grid=(s // tm,),
        in_specs=[
            pl.BlockSpec((tm, D_MODEL), lambda i: (i, 0)),
            heads(MOBA_HEADS), heads(DIFF_HEADS), heads(DSA_HEADS),
            whole((D_MODEL, D_MODEL)), whole((1, D_MODEL)), whole((D_MODEL, CROSS_W)),
            whole((1, LANES)), whole((n_mem, CROSS_W)), whole((n_mem, CROSS_W)),
            whole((CROSS_W, D_MODEL)),
        ],
        out_specs=pl.BlockSpec((tm, D_MODEL), lambda i: (i, 0)),
        out_shape=jax.ShapeDtypeStruct((s, D_MODEL), F32),
        compiler_params=_cparams(("arbitrary",)),
        name="out_proj_cross",
    )(x, om, od, osa, wout, gc, wq, gq, ck, cv, wo)


FFN_HALO = 16
FFN_ROW_CHUNK = 256

def _ffn_kernel(x_ref, xp_ref, g_ref, wg_ref, wv_ref, cwg_ref, cwv_ref, cbg_ref, cbv_ref,
                wd_ref, o_ref, h_scr, *, tm):
    i = pl.program_id(0)
    f = pl.program_id(1)

    @pl.when(f == 0)
    def _():
        prev = _rms_rows(xp_ref[...], g_ref[...])
        h_scr[0:FFN_HALO, :] = jnp.where(i > 0, prev, 0.0).astype(BF16)
        h_scr[FFN_HALO:, :] = _rms_rows(x_ref[...], g_ref[...]).astype(BF16)
        o_ref[...] = x_ref[...]

    for r0 in range(0, tm, FFN_ROW_CHUNK):
        h = h_scr[r0:r0 + FFN_ROW_CHUNK + FFN_HALO, :]

        def conv(w_ref, cw_ref, cb_ref):
            u = jnp.dot(h, w_ref[...], preferred_element_type=F32)
            cw = cw_ref[...]
            uc = (cw[0:1] * pltpu.roll(u, 2, 0) + cw[1:2] * pltpu.roll(u, 1, 0) + cw[2:3] * u
                  + cb_ref[...])
            return uc[FFN_HALO:, :]

        gate = conv(wg_ref, cwg_ref, cbg_ref)
        val = conv(wv_ref, cwv_ref, cbv_ref)
        act = (gate * jax.nn.sigmoid(gate) * val).astype(BF16)
        o_ref[r0:r0 + FFN_ROW_CHUNK, :] += jnp.dot(act, wd_ref[...], preferred_element_type=F32)


def _ffn(x, g, w_up_p, cw_p, cb_p, w_down_p, s):
    tm = min(s, 512)
    nf = D_FF_PAD // FF_TILE
    halo_blocks = tm // FFN_HALO
    return pl.pallas_call(
        functools.partial(_ffn_kernel, tm=tm),
        grid=(s // tm, nf),
        in_specs=[
            pl.BlockSpec((tm, D_MODEL), lambda i, f: (i, 0)),
            pl.BlockSpec((FFN_HALO, D_MODEL), lambda i, f: (jnp.maximum(i * halo_blocks - 1, 0), 0)),
            pl.BlockSpec((1, D_MODEL), lambda i, f: (0, 0)),
            pl.BlockSpec((D_MODEL, FF_TILE), lambda i, f: (0, f)),
            pl.BlockSpec((D_MODEL, FF_TILE), lambda i, f: (0, f + nf)),
            pl.BlockSpec((3, FF_TILE), lambda i, f: (0, f)),
            pl.BlockSpec((3, FF_TILE), lambda i, f: (0, f + nf)),
            pl.BlockSpec((1, FF_TILE), lambda i, f: (0, f)),
            pl.BlockSpec((1, FF_TILE), lambda i, f: (0, f + nf)),
            pl.BlockSpec((FF_TILE, D_MODEL), lambda i, f: (f, 0)),
        ],
        out_specs=pl.BlockSpec((tm, D_MODEL), lambda i, f: (i, 0)),
        out_shape=jax.ShapeDtypeStruct((s, D_MODEL), F32),
        scratch_shapes=[pltpu.VMEM((tm + FFN_HALO, D_MODEL), BF16)],
        compiler_params=_cparams(("arbitrary", "arbitrary")),
        name="ffn",
    )(x, x, g, w_up_p, w_up_p, cw_p, cw_p, cb_p, cb_p, w_down_p)


CAST_ROWS = 128


def _cast_kernel(w_ref, o_ref, *, rows_in):
    live = pl.program_id(0) * CAST_ROWS < rows_in
    o_ref[...] = jnp.where(live, w_ref[0], 0.0).astype(BF16)


def _cast_layer(w, layer, rows_out=None):
    _, rows, n = w.shape
    rows_out = rows if rows_out is None else rows_out
    last = rows // CAST_ROWS - 1
    return pl.pallas_call(
        functools.partial(_cast_kernel, rows_in=rows),
        grid=(rows_out // CAST_ROWS,),
        in_specs=[pl.BlockSpec((1, CAST_ROWS, n), lambda i: (layer, jnp.minimum(i, last), 0))],
        out_specs=pl.BlockSpec((CAST_ROWS, n), lambda i: (i, 0)),
        out_shape=jax.ShapeDtypeStruct((rows_out, n), BF16),
        compiler_params=_cparams(("arbitrary",)),
        name="cast_weight",
    )(w)


def _cast_up_kernel(w_ref, o_ref):
    w = w_ref[0]
    pad = jnp.zeros((w.shape[0], D_FF_PAD - D_FF), BF16)
    o_ref[:, 0:D_FF] = w[:, 0:D_FF].astype(BF16)
    o_ref[:, D_FF:D_FF_PAD] = pad
    o_ref[:, D_FF_PAD:D_FF_PAD + D_FF] = w[:, D_FF:].astype(BF16)
    o_ref[:, D_FF_PAD + D_FF:] = pad


def _cast_up(w_up, layer):
    return pl.pallas_call(
        _cast_up_kernel,
        grid=(D_MODEL // CAST_ROWS,),
        in_specs=[pl.BlockSpec((1, CAST_ROWS, 2 * D_FF), lambda i: (layer, i, 0))],
        out_specs=pl.BlockSpec((CAST_ROWS, 2 * D_FF_PAD), lambda i: (i, 0)),
        out_shape=jax.ShapeDtypeStruct((D_MODEL, 2 * D_FF_PAD), BF16),
        compiler_params=_cparams(("arbitrary",)),
        name="cast_w_up",
    )(w_up)


def _split_pad(a):
    r = a.shape[0]
    halves = jnp.pad(a.reshape(r, 2, D_FF), ((0, 0), (0, 0), (0, D_FF_PAD - D_FF)))
    return halves.reshape(r, 2 * D_FF_PAD)


def kernel(x, mem, positions, attn_norm, w_in, moba_qk_gain, diff_qk_gain, diff_lambda,
           diff_subln, dsa_qk_gain, w_out, cross_norm, mem_norm, cross_wq, cross_wkv,
           cross_qk_gain, cross_wo, ffn_norm, ffn_w_up, ffn_conv_w, ffn_conv_b, ffn_w_down):
    b, s, _ = x.shape
    assert b == 1 and s % DSA_TQ == 0 and s % MOBA_BLOCK == 0
    xs = x.reshape(s, D_MODEL)
    mem2d = mem.reshape(mem.shape[1], D_MODEL)
    tabs = _rope_tables(positions, s)
    row = lambda v: v.reshape(1, -1)
    for l in range(DEPTH):
        gains = jnp.concatenate([
            moba_qk_gain[l], dsa_qk_gain[l], jnp.tile(diff_qk_gain[l], (1, 2)),
            jnp.zeros((2, LANES), F32)], axis=0)
        main, tail = _in_proj(xs, row(attn_norm[l]), w_in, l, tabs, gains, s)
        o_moba = _moba(main, s)
        o_diff = _diff(main, diff_lambda[l], row(diff_subln[l]), l, s)
        o_dsa = _dsa(main, tail, s)
        ck, cv = _memkv(mem2d, row(mem_norm[l]), _cast_layer(cross_wkv, l), row(cross_qk_gain[l, 1]))
        xs = _mid(xs, o_moba, o_diff, o_dsa, _cast_layer(w_out, l), row(cross_norm[l]),
                  _cast_layer(cross_wq, l), row(cross_qk_gain[l, 0]), ck, cv,
                  _cast_layer(cross_wo, l), s)
        w_up_p = _cast_up(ffn_w_up, l)
        cw_p = _split_pad(ffn_conv_w[l])
        cb_p = _split_pad(row(ffn_conv_b[l]))
        w_down_p = _cast_layer(ffn_w_down, l, D_FF_PAD)
        xs = _ffn(xs, row(ffn_norm[l]), w_up_p, cw_p, cb_p, w_down_p, s)
    return xs.reshape(b, s, D_MODEL)
```

```python
import functools
import math

import jax
import jax.numpy as jnp
from jax import lax
from jax.experimental import pallas as pl
from jax.experimental.pallas import tpu as pltpu

F32 = jnp.float32
BF16 = jnp.bfloat16
I32 = jnp.int32

D_MODEL = 2048
DEPTH = 2
HEAD_DIM = 128
MOBA_HEADS = 6
DIFF_HEADS = 4
DSA_HEADS = 6
MOBA_BLOCK = 256
MOBA_TOPK = 3
DIFF_QK_DIM = 64
DSA_TOPK = 256
IDX_HEADS = 8
IDX_DIM = 64
MEM_LEN = 256
CROSS_HEADS = 4
CROSS_W = CROSS_HEADS * HEAD_DIM
D_FF = 5504
ROPE_THETA = 10000.0
RMS_EPS = 1e-6

LANES = 128
PROJ_TN = 256
PROJ_ROW_CHUNK = 128
D_IN = 6728
D_IN_PAD = 6912
D_IN_TAIL = D_IN - (D_IN_PAD - 256)
N_PROJ_TILES = D_IN_PAD // PROJ_TN
N_HEAD_BLOCKS = D_IN_PAD // LANES
FF_TILE = 512
D_FF_PAD = 5632
NEG = -1e30
LOG2E = 1.4426950408889634
FLASH_WIDTH = 4
MOBA_HEADS_PER_STEP = 3
DSA_HEADS_PER_STEP = 3
DIFF_HEADS_PER_STEP = 2
INT_MIN = -(2 ** 31)
INT_MAX = 2 ** 31 - 1
SEARCH_LINEAR_STEPS = 40
VMEM_LIMIT = 56 * 1024 * 1024

BLK_MQ, BLK_MK, BLK_MV = 0, 6, 12
BLK_DQ, BLK_DK, BLK_DV = 18, 22, 26
BLK_SQ, BLK_SK, BLK_SV = 30, 36, 42
BLK_IQ, BLK_IK = 48, 52


def _cparams(sem):
    return pltpu.CompilerParams(dimension_semantics=sem, vmem_limit_bytes=VMEM_LIMIT)


def _rms_rows(x, g):
    return x * lax.rsqrt(jnp.mean(x * x, axis=-1, keepdims=True) + RMS_EPS) * g


def _dot_t(a, b):
    return lax.dot_general(a, b, (((1,), (1,)), ((), ())), preferred_element_type=F32)


def _rope_tab_kernel(pos_ref, inv128_ref, inv64_ref, c128_ref, s128_ref, c64_ref, s64_ref):
    p = pos_ref[...].astype(F32)
    lane = lax.broadcasted_iota(I32, (1, LANES), 1)
    a = p * inv128_ref[...]
    sa = jnp.sin(a)
    c128_ref[...] = jnp.cos(a)
    s128_ref[...] = jnp.where(lane < 64, -sa, sa)
    b = p * inv64_ref[...]
    sb = jnp.sin(b)
    c64_ref[...] = jnp.cos(b)
    s64_ref[...] = jnp.where((lane % 64) < 32, -sb, sb)


def _rope_tables(positions, s):
    tm = min(s, 1024)
    inv_a = ROPE_THETA ** (-(jnp.arange(64, dtype=F32) * 2.0 / 128))
    inv_b = ROPE_THETA ** (-(jnp.arange(32, dtype=F32) * 2.0 / 64))
    inv128 = jnp.tile(inv_a, 2).reshape(1, LANES)
    inv64 = jnp.tile(inv_b, 4).reshape(1, LANES)
    pos = positions.reshape(s, 1)
    tab = jax.ShapeDtypeStruct((s, LANES), F32)
    row = pl.BlockSpec((tm, LANES), lambda i: (i, 0))
    cst = pl.BlockSpec((1, LANES), lambda i: (0, 0))
    return pl.pallas_call(
        _rope_tab_kernel,
        grid=(s // tm,),
        in_specs=[pl.BlockSpec((tm, 1), lambda i: (i, 0)), cst, cst],
        out_specs=[row, row, row, row],
        out_shape=[tab, tab, tab, tab],
        compiler_params=_cparams(("arbitrary",)),
        name="rope_tables",
    )(pos, inv128, inv64)


def _proj_kernel(x_ref, g_ref, w_ref, c128_ref, s128_ref, c64_ref, s64_ref, gains_ref,
                 main_ref, tail_ref, h_scr):
    j = pl.program_id(1)

    @pl.when(j == 0)
    def _():
        h_scr[...] = _rms_rows(x_ref[...], g_ref[...]).astype(BF16)

    lane = lax.broadcasted_iota(I32, (1, LANES), 1)
    lo = lane < 64
    tm = h_scr.shape[0]
    chunks = [slice(r0, r0 + PROJ_ROW_CHUNK) for r0 in range(0, tm, PROJ_ROW_CHUNK)]

    def product(rs, w):
        return jnp.dot(h_scr[rs, :], w, preferred_element_type=F32)

    def rope128(y, rs):
        return y * c128_ref[rs, :] + pltpu.roll(y, 64, 1) * s128_ref[rs, :]

    def rope64(y, rs):
        partner = jnp.where((lane % 64) < 32, pltpu.roll(y, 96, 1), pltpu.roll(y, 32, 1))
        return y * c64_ref[rs, :] + partner * s64_ref[rs, :]

    def norm64(y, g):
        y2 = y * y
        s_lo = jnp.sum(jnp.where(lo, y2, 0.0), axis=-1, keepdims=True)
        s_hi = jnp.sum(jnp.where(lo, 0.0, y2), axis=-1, keepdims=True)
        ms = jnp.where(lo, s_lo, s_hi) * (1.0 / 64)
        return y * lax.rsqrt(ms + RMS_EPS) * g

    def emit(fn):
        w = w_ref[0].astype(BF16)
        for rs in chunks:
            r = product(rs, w)
            for c in range(2):
                main_ref[c, rs, :] = fn(r[:, c * LANES:(c + 1) * LANES], rs).astype(BF16)

    def qk128(row, post=None):
        fn = lambda y, rs: rope128(_rms_rows(y, gains_ref[row:row + 1, :]), rs)
        return fn if post is None else (lambda y, rs: fn(y, rs) * post)

    def qk64(row, post=None):
        fn = lambda y, rs: rope64(norm64(y, gains_ref[row:row + 1, :]), rs)
        return fn if post is None else (lambda y, rs: fn(y, rs) * post)

    ident = lambda y, rs: y
    q128 = HEAD_DIM ** -0.5 * LOG2E
    q64 = DIFF_QK_DIM ** -0.5 * LOG2E
    segments = (
        (0, 3, qk128(0, q128)), (3, 6, qk128(1)), (6, 9, ident),
        (9, 11, qk64(4, q64)), (11, 13, qk64(5)), (13, 15, ident),
        (15, 18, qk128(2, q128)), (18, 21, qk128(3)), (21, 24, ident),
        (24, 26, rope64),
    )
    for first, last, fn in segments:
        pl.when((j >= first) & (j < last))(functools.partial(emit, fn))

    @pl.when(j == N_PROJ_TILES - 1)
    def _():
        w = w_ref[0, :, 0:LANES].astype(BF16)
        for rs in chunks:
            y = jnp.where(lane < D_IN_TAIL, product(rs, w), 0.0)
            kk = jnp.where(lo, rope64(y, rs), 0.0)
            main_ref[0, rs, :] = kk.astype(BF16)
            main_ref[1, rs, :] = pltpu.roll(kk, 64, 1).astype(BF16)
            tail_ref[rs, :] = (y * (IDX_HEADS ** -0.5)) * (IDX_DIM ** -0.5)


def _in_proj(x, g, w_in, layer, tabs, gains, s):
    tm = min(s, 1024)
    c128, s128, c64, s64 = tabs
    row = pl.BlockSpec((tm, LANES), lambda i, j: (i, 0))
    return pl.pallas_call(
        _proj_kernel,
        grid=(s // tm, N_PROJ_TILES),
        in_specs=[
            pl.BlockSpec((tm, D_MODEL), lambda i, j: (i, 0)),
            pl.BlockSpec((1, D_MODEL), lambda i, j: (0, 0)),
            pl.BlockSpec((1, D_MODEL, PROJ_TN), lambda i, j: (layer, 0, j)),
            row, row, row, row,
            pl.BlockSpec((8, LANES), lambda i, j: (0, 0)),
        ],
        out_specs=[
            pl.BlockSpec((2, tm, LANES), lambda i, j: (j, i, 0)),
            pl.BlockSpec((tm, LANES), lambda i, j: (i, 0)),
        ],
        out_shape=[
            jax.ShapeDtypeStruct((N_HEAD_BLOCKS, s, LANES), BF16),
            jax.ShapeDtypeStruct((s, LANES), F32),
        ],
        scratch_shapes=[pltpu.VMEM((tm, D_MODEL), BF16)],
        compiler_params=_cparams(("arbitrary", "arbitrary")),
        name="in_proj",
    )(x, g, w_in, c128, s128, c64, s64, gains)


def _flash_slab(q, k_slab, v_slab, masks, carry, bias=None):
    m, l, acc = carry
    sc = _dot_t(q, k_slab)
    if bias is not None:
        sc = sc + bias
    if masks is not None:
        tk = sc.shape[1] // len(masks)
        sc = jnp.concatenate([jnp.where(mk, sc[:, u * tk:(u + 1) * tk], NEG)
                              for u, mk in enumerate(masks)], axis=1)
    m_new = jnp.maximum(m, jnp.max(sc, axis=-1, keepdims=True))
    p = jnp.exp2(sc - m_new)
    alpha = jnp.exp2(m - m_new)
    l = alpha * l + jnp.sum(p, axis=-1, keepdims=True)
    acc = alpha * acc + jnp.dot(p.astype(BF16), v_slab, preferred_element_type=F32)
    return m_new, l, acc


def _flash_init(rows):
    return (jnp.full((rows, 1), NEG, F32), jnp.zeros((rows, 1), F32),
            jnp.zeros((rows, HEAD_DIM), F32))


def _flash_loop(streams, first_slab, n_slabs, states, shared_masks=None, shared_bias=None):
    def slab(g, sts):
        common = None if shared_masks is None else shared_masks(g)
        bias = None if shared_bias is None else shared_bias(g)
        out = []
        for (q, load_slab), st in zip(streams, sts):
            k_slab, v_slab, masks = load_slab(g)
            out.append(_flash_slab(q, k_slab, v_slab, masks if common is None else common, st, bias))
        return tuple(out)
    return lax.fori_loop(first_slab, n_slabs, slab, tuple(states))


def _flash_finish(state):
    _, l, acc = state
    return acc / l


def _moba_kernel(q_ref, k_ref, v_ref, o_ref, kbar_scr, *, nb, n_sel):
    blk = MOBA_BLOCK
    i = pl.program_id(1)

    @pl.when(i == 0)
    def _():
        kbar_scr[...] = jnp.zeros_like(kbar_scr)
        for hh in range(MOBA_HEADS_PER_STEP):
            def mean_block(n, c, hh=hh):
                kk = k_ref[hh, pl.ds(pl.multiple_of(n * blk, blk), blk), :].astype(F32)
                kbar_scr[hh, pl.ds(n, 1), :] = jnp.sum(kk, axis=0, keepdims=True) * (1.0 / blk)
                return c
            lax.fori_loop(0, nb, mean_block, 0)

    blk_id = lax.broadcasted_iota(I32, (LANES, blk), 0)
    valid = blk_id < i
    bl = lax.broadcasted_iota(I32, (blk, LANES), 1)

    def stream(hh):
        q = q_ref[hh]
        gate = _dot_t(kbar_scr[hh].astype(BF16), q)
        g = jnp.where(valid, gate, -jnp.inf)
        sel_t = jnp.zeros((LANES, blk), F32)
        for _ in range(n_sel):
            mx = jnp.max(g, axis=0, keepdims=True)
            first = jnp.min(jnp.where(g == mx, blk_id, LANES), axis=0, keepdims=True)
            hit = blk_id == first
            sel_t = jnp.where(hit, 1.0, sel_t)
            g = jnp.where(hit, -jnp.inf, g)
        sel = jnp.where(valid, sel_t, 0.0).T

        lim_tab = jnp.where(bl == i, row, jnp.where(sel > 0.0, blk - 1.0, -1.0))

        def load(g):
            masks = []
            for u in range(FLASH_WIDTH):
                j = g * FLASH_WIDTH + u
                lim = jnp.max(jnp.where(bl == j, lim_tab, -1.0), axis=-1, keepdims=True)
                masks.append(cols <= lim)
            sl = pl.ds(pl.multiple_of(g * slab_keys, slab_keys), slab_keys)
            return k_ref[hh, sl, :], v_ref[hh, sl, :], masks
        return q, load

    row = lax.broadcasted_iota(I32, (blk, LANES), 0).astype(F32)
    cols = lax.broadcasted_iota(I32, (blk, blk), 1).astype(F32)
    slab_keys = FLASH_WIDTH * blk
    streams = [stream(hh) for hh in range(MOBA_HEADS_PER_STEP)]
    n_slabs = (i + FLASH_WIDTH) // FLASH_WIDTH
    states = _flash_loop(streams, 0, n_slabs, [_flash_init(blk) for _ in streams])
    for hh, st in enumerate(states):
        o_ref[hh] = _flash_finish(st).astype(BF16)


def _moba(main, s):
    blk = MOBA_BLOCK
    nb = s // blk
    hps = MOBA_HEADS_PER_STEP
    assert nb <= LANES and nb % FLASH_WIDTH == 0 and MOBA_HEADS % hps == 0
    n_sel = max(1, min(MOBA_TOPK, nb - 1))
    return pl.pallas_call(
        functools.partial(_moba_kernel, nb=nb, n_sel=n_sel),
        grid=(MOBA_HEADS // hps, nb),
        in_specs=[
            pl.BlockSpec((hps, blk, LANES), lambda h, i: (BLK_MQ // hps + h, i, 0)),
            pl.BlockSpec((hps, s, LANES), lambda h, i: (BLK_MK // hps + h, 0, 0)),
            pl.BlockSpec((hps, s, LANES), lambda h, i: (BLK_MV // hps + h, 0, 0)),
        ],
        out_specs=pl.BlockSpec((hps, blk, LANES), lambda h, i: (h, i, 0)),
        out_shape=jax.ShapeDtypeStruct((MOBA_HEADS, s, LANES), BF16),
        scratch_shapes=[pltpu.VMEM((hps, LANES, LANES), F32)],
        compiler_params=_cparams(("arbitrary", "arbitrary")),
        name="moba",
    )(main, main, main)


def _diff_kernel(q_ref, k_ref, v_ref, lam_ref, g_ref, o_ref, *, tq, lam_init):
    i = pl.program_id(1)
    lane = lax.broadcasted_iota(I32, (tq, LANES), 1)
    slab_keys = FLASH_WIDTH * tq
    row = lax.broadcasted_iota(I32, (2 * tq, 1), 0) & (tq - 1)
    cols = lax.broadcasted_iota(I32, (2 * tq, tq), 1)

    def tail_masks(g):
        masks = []
        for u in range(FLASH_WIDTH):
            j = g * FLASH_WIDTH + u
            masks.append(cols <= jnp.where(j < i, tq - 1, jnp.where(j == i, row, -1)))
        return masks

    def stream(hh):
        q = q_ref[hh]
        zero = jnp.zeros_like(q)
        qq = jnp.concatenate([jnp.where(lane < 64, q, zero), jnp.where(lane < 64, zero, q)], axis=0)

        def load(g):
            sl = pl.ds(pl.multiple_of(g * slab_keys, slab_keys), slab_keys)
            return k_ref[hh, sl, :], v_ref[hh, sl, :], None
        return qq, load

    streams = [stream(hh) for hh in range(DIFF_HEADS_PER_STEP)]
    full = i // FLASH_WIDTH
    states = _flash_loop(streams, 0, full, [_flash_init(2 * tq) for _ in streams])
    states = _flash_loop(streams, full, full + 1, states, shared_masks=tail_masks)
    lf = lam_ref[...]
    lam = (jnp.exp(jnp.sum(lf[0:1] * lf[1:2], axis=-1, keepdims=True))
           - jnp.exp(jnp.sum(lf[2:3] * lf[3:4], axis=-1, keepdims=True)) + lam_init)
    for hh, st in enumerate(states):
        o = _flash_finish(st)
        a = o[:tq] - lam * o[tq:]
        o_ref[hh] = (_rms_rows(a, g_ref[...]) * (1.0 - lam_init)).astype(BF16)


def _diff(main, lam_params, subln, layer, s):
    tq = min(s, 256)
    hps = DIFF_HEADS_PER_STEP
    assert (s // tq) % FLASH_WIDTH == 0 and DIFF_HEADS % hps == 0
    lam_init = 0.8 - 0.6 * math.exp(-0.3 * layer)
    return pl.pallas_call(
        functools.partial(_diff_kernel, tq=tq, lam_init=lam_init),
        grid=(DIFF_HEADS // hps, s // tq),
        in_specs=[
            pl.BlockSpec((hps, tq, LANES), lambda h, i: (BLK_DQ // hps + h, i, 0)),
            pl.BlockSpec((hps, s, LANES), lambda h, i: (BLK_DK // hps + h, 0, 0)),
            pl.BlockSpec((hps, s, LANES), lambda h, i: (BLK_DV // hps + h, 0, 0)),
            pl.BlockSpec((4, DIFF_QK_DIM), lambda h, i: (0, 0)),
            pl.BlockSpec((1, LANES), lambda h, i: (0, 0)),
        ],
        out_specs=pl.BlockSpec((hps, tq, LANES), lambda h, i: (h, i, 0)),
        out_shape=jax.ShapeDtypeStruct((DIFF_HEADS, s, LANES), BF16),
        compiler_params=_cparams(("arbitrary", "arbitrary")),
        name="diff_attn",
    )(main, main, main, lam_params, subln)


DSA_TQ = 256


def _sort_key(sc):
    sc = jnp.where(sc == 0.0, 0.0, sc)
    b = lax.bitcast_convert_type(sc, I32)
    return b ^ ((b >> 31) & 0x7FFFFFFF)


def _dsa_kernel(qi_ref, ki_ref, w_ref, q_ref, k_ref, v_ref, o_ref,
                key_scr, keyt_scr, wb_scr, thr_scr, *, n_keep):
    t = DSA_TQ
    i = pl.program_id(0)
    w = w_ref[...]
    for h in range(IDX_HEADS):
        wb_scr[h] = jnp.broadcast_to(w[:, 64 + h:65 + h], (t, t))

    def tile(j):
        return pl.ds(pl.multiple_of(j * t, t), t)

    q_idx = qi_ref[...].reshape(IDX_HEADS // 2 * t, LANES)

    def index_scores(j):
        s_even = _dot_t(q_idx, ki_ref[0, tile(j), :])
        s_odd = _dot_t(q_idx, ki_ref[1, tile(j), :])
        sc = jnp.zeros((t, t), F32)
        for b in range(IDX_HEADS // 2):
            rs = slice(b * t, (b + 1) * t)
            sc = sc + wb_scr[2 * b] * jnp.maximum(s_even[rs], 0.0)
            sc = sc + wb_scr[2 * b + 1] * jnp.maximum(s_odd[rs], 0.0)
        return sc

    rows = lax.broadcasted_iota(I32, (t, t), 0)
    cols = lax.broadcasted_iota(I32, (t, t), 1)
    causal = cols <= rows

    def put_keys(j, keys):
        key_scr[:, tile(j)] = keys
        keyt_scr[tile(j), :] = keys.T

    def fill_past(j, c):
        put_keys(j, _sort_key(index_scores(j)))
        return c

    def fill_pair(jj, c):
        return fill_past(2 * jj + 1, fill_past(2 * jj, c))
    lax.fori_loop(0, i // 2, fill_pair, 0)
    lax.fori_loop(2 * (i // 2), i, fill_past, 0)
    put_keys(i, jnp.where(causal, _sort_key(index_scores(i)), INT_MIN))
    n_steps = (i + FLASH_WIDTH) // FLASH_WIDTH

    def fill_pad(j, c):
        key_scr[:, tile(j)] = jnp.full((t, t), INT_MIN, I32)
        return c
    lax.fori_loop(i + 1, n_steps * FLASH_WIDTH, fill_pad, 0)

    def count_keys(preds):
        def step(j, cnts):
            kk = keyt_scr[tile(j), :].reshape(t // 8, 8, t)
            return tuple(c + jnp.sum(jnp.where(p(kk), 1, 0), axis=0) for c, p in zip(cnts, preds))

        def pair(jj, cnts):
            return step(2 * jj + 1, step(2 * jj, cnts))
        zero = jnp.zeros((8, t), I32)
        n_pairs = (i + 1) // 2
        cnts = lax.fori_loop(0, n_pairs, pair, tuple(zero for _ in preds))
        cnts = lax.fori_loop(2 * n_pairs, i + 1, step, cnts)
        return [jnp.sum(c, axis=0, keepdims=True) for c in cnts]

    def ge(c):
        return lambda kk: kk >= c

    def extremes(j, c, own):
        kk = keyt_scr[tile(j), :].reshape(t // 8, 8, t)
        floor = jnp.where(kk == INT_MIN, INT_MAX, kk) if own else kk
        return jnp.minimum(c[0], jnp.min(floor, axis=0)), jnp.maximum(c[1], jnp.max(kk, axis=0))
    mn, mx = lax.fori_loop(0, i, functools.partial(extremes, own=False),
                           (jnp.full((8, t), INT_MAX, I32), jnp.full((8, t), INT_MIN, I32)))
    mn, mx = extremes(i, (mn, mx), own=True)
    lo = jnp.broadcast_to(jnp.min(mn, axis=0, keepdims=True), (8, t))
    hi = jnp.broadcast_to(jnp.max(mx, axis=0, keepdims=True), (8, t)) + 1
    zero_key = jnp.zeros((8, t), I32)
    n_ge0, n_gt0 = count_keys([ge(zero_key), ge(zero_key + 1)])
    n_adm = i * t + lax.broadcasted_iota(I32, (1, t), 1) + 1
    few = n_adm < n_keep
    at_zero = (n_gt0 < n_keep) & (n_ge0 >= n_keep)
    above = n_gt0 >= n_keep
    lo_cnt = jnp.broadcast_to(jnp.where(above, n_gt0, jnp.where(at_zero, n_ge0, n_adm)), (8, t))
    lo = jnp.where(above, jnp.maximum(lo, 1), jnp.where(at_zero, 0, lo))
    hi = jnp.where(above, hi, jnp.where(at_zero, 1, jnp.minimum(hi, 0)))

    def finished(lo, hi, lo_cnt):
        return jnp.where(few | (lo_cnt == n_keep) | (hi - lo == 1), 1, 0)

    def key_to_score(k):
        return lax.bitcast_convert_type(k ^ ((k >> 31) & 0x7FFFFFFF), F32)

    def narrow(carry):
        it, _, lo, hi, lo_cnt = carry
        all_done = jnp.min(finished(lo, hi, lo_cnt))
        mid = 0.5 * key_to_score(lo) + 0.5 * key_to_score(hi)
        b = lax.bitcast_convert_type(mid, I32)
        by_score = b ^ ((b >> 31) & 0x7FFFFFFF)
        by_key = (lo >> 1) + (hi >> 1) + (lo & hi & 1)
        cand = jnp.where(it < SEARCH_LINEAR_STEPS, by_score, by_key)
        cand = jnp.minimum(jnp.maximum(cand, lo + 1), hi - 1)
        cnt, = count_keys([ge(cand)])
        live = finished(lo, hi, lo_cnt) == 0
        up = live & (cnt >= n_keep)
        down = live & (cnt < n_keep)
        lo, lo_cnt, hi = jnp.where(up, cand, lo), jnp.where(up, cnt, lo_cnt), jnp.where(down, cand, hi)
        return it + 1, all_done, lo, hi, lo_cnt

    first = (jnp.int32(0), jnp.int32(0), lo, hi, lo_cnt)
    _, _, lo, hi, lo_cnt = lax.while_loop(lambda c: c[1] == 0, narrow, first)
    thr_t = jnp.where(few, INT_MIN + 1, lo)
    thr_scr[...] = jnp.broadcast_to(thr_t[0:1], (t, t)).T

    surplus = jnp.max(jnp.where(few, 0, jnp.where(lo_cnt > n_keep, 1, 0)))

    @pl.when(surplus > 0)
    def _():
        n_gt_t, = count_keys([lambda kk: kk > thr_t])
        need_t = (n_keep - n_gt_t).astype(F32)
        need = jnp.broadcast_to(need_t, (t, t)).T[:, 0:1]
        thr = thr_scr[...]
        upper = jnp.where(rows <= cols, 1.0, 0.0).astype(BF16)

        def drop_late_ties(j, before):
            kk = key_scr[:, tile(j)]
            tie = kk == thr
            seen = before + jnp.dot(jnp.where(tie, 1.0, 0.0).astype(BF16), upper,
                                    preferred_element_type=F32)
            key_scr[:, tile(j)] = jnp.where(tie, jnp.where(seen > need, INT_MIN, kk), kk)
            return seen[:, t - 1:t]
        lax.fori_loop(0, i + 1, drop_late_ties, jnp.zeros((t, 1), F32))

    slab_keys = FLASH_WIDTH * t

    def key_to_bias(j, c):
        keep = key_scr[:, tile(j)] >= thr_scr[...]
        key_scr[:, tile(j)] = lax.bitcast_convert_type(jnp.where(keep, 0.0, NEG), I32)
        return c
    lax.fori_loop(0, n_steps * FLASH_WIDTH, key_to_bias, 0)

    def admitted_bias(g):
        sl = pl.ds(pl.multiple_of(g * slab_keys, slab_keys), slab_keys)
        return lax.bitcast_convert_type(key_scr[:, sl], F32)

    def head_group(hg, c):
        def stream(h):
            def load(g):
                sl = pl.ds(pl.multiple_of(g * slab_keys, slab_keys), slab_keys)
                return k_ref[h, sl, :], v_ref[h, sl, :], None
            return q_ref[h], load

        heads = [hg * DSA_HEADS_PER_STEP + u for u in range(DSA_HEADS_PER_STEP)]
        states = _flash_loop([stream(h) for h in heads], 0, n_steps, [_flash_init(t) for _ in heads],
                             shared_bias=admitted_bias)
        for h, st in zip(heads, states):
            o_ref[h] = _flash_finish(st).astype(BF16)
        return c
    lax.fori_loop(0, DSA_HEADS // DSA_HEADS_PER_STEP, head_group, 0)


def _dsa(main, tail, s):
    t = DSA_TQ
    n_keep = min(DSA_TOPK, s // 4)
    slab = FLASH_WIDTH * t
    s_pad = -(-s // slab) * slab
    resident = dict(pipeline_mode=pl.Buffered(1))
    return pl.pallas_call(
        functools.partial(_dsa_kernel, n_keep=n_keep),
        grid=(s // t,),
        in_specs=[
            pl.BlockSpec((IDX_HEADS // 2, t, LANES), lambda i: (BLK_IQ // 4, i, 0)),
            pl.BlockSpec((2, s, LANES), lambda i: (BLK_IK // 2, 0, 0), **resident),
            pl.BlockSpec((t, LANES), lambda i: (i, 0)),
            pl.BlockSpec((DSA_HEADS, t, LANES), lambda i: (BLK_SQ // 6, i, 0)),
            pl.BlockSpec((DSA_HEADS, s, LANES), lambda i: (BLK_SK // 6, 0, 0), **resident),
            pl.BlockSpec((DSA_HEADS, s, LANES), lambda i: (BLK_SV // 6, 0, 0), **resident),
        ],
        out_specs=pl.BlockSpec((DSA_HEADS, t, LANES), lambda i: (0, i, 0)),
        out_shape=jax.ShapeDtypeStruct((DSA_HEADS, s, LANES), BF16),
        scratch_shapes=[
            pltpu.VMEM((t, s_pad), I32),
            pltpu.VMEM((s, t), I32),
            pltpu.VMEM((IDX_HEADS, t, t), F32),
            pltpu.VMEM((t, t), I32),
        ],
        compiler_params=_cparams(("arbitrary",)),
        name="dsa",
    )(main, main, tail, main, main, main)


def _memkv_kernel(mem_ref, g_ref, wkv_ref, gk_ref, ck_ref, cv_ref):
    m = _rms_rows(mem_ref[...], g_ref[...]).astype(BF16)
    kv = jnp.dot(m, wkv_ref[...], preferred_element_type=F32)
    for h in range(CROSS_HEADS):
        sl = slice(h * HEAD_DIM, (h + 1) * HEAD_DIM)
        ck_ref[:, sl] = _rms_rows(kv[:, sl], gk_ref[...]).astype(BF16)
    cv_ref[...] = kv[:, CROSS_W:].astype(BF16)


def _memkv(mem2d, g, wkv, gk):
    n = mem2d.shape[0]
    out = jax.ShapeDtypeStruct((n, CROSS_W), BF16)
    return pl.pallas_call(
        _memkv_kernel,
        out_shape=[out, out],
        compiler_params=pltpu.CompilerParams(vmem_limit_bytes=VMEM_LIMIT),
        name="mem_kv",
    )(mem2d, g, wkv, gk)


def _mid_kernel(x_ref, om_ref, od_ref, os_ref, wout_ref, gc_ref, wq_ref, gq_ref,
                ck_ref, cv_ref, wo_ref, o_ref):
    heads = ([om_ref[h] for h in range(MOBA_HEADS)] + [od_ref[h] for h in range(DIFF_HEADS)]
             + [os_ref[h] for h in range(DSA_HEADS)])
    mixed = jnp.concatenate(heads, axis=-1)
    x1 = x_ref[...] + jnp.dot(mixed, wout_ref[...], preferred_element_type=F32)
    hq = _rms_rows(x1, gc_ref[...]).astype(BF16)
    cq = jnp.dot(hq, wq_ref[...], preferred_element_type=F32)
    scale = HEAD_DIM ** -0.5
    outs = []
    for h in range(CROSS_HEADS):
        sl = slice(h * HEAD_DIM, (h + 1) * HEAD_DIM)
        qh = _rms_rows(cq[:, sl], gq_ref[...]).astype(BF16)
        sc = _dot_t(qh, ck_ref[:, sl]) * scale
        m = jnp.max(sc, axis=-1, keepdims=True)
        p = jnp.exp(sc - m)
        l = jnp.sum(p, axis=-1, keepdims=True)
        outs.append(jnp.dot(p.astype(BF16), cv_ref[:, sl], preferred_element_type=F32) / l)
    co = jnp.concatenate(outs, axis=-1).astype(BF16)
    o_ref[...] = x1 + jnp.dot(co, wo_ref[...], preferred_element_type=F32)


def _mid(x, om, od, osa, wout, gc, wq, gq, ck, cv, wo, s):
    tm = min(s, 512)
    n_mem = ck.shape[0]
    whole = lambda shape: pl.BlockSpec(shape, lambda i: (0,) * len(shape), pipeline_mode=pl.Buffered(1))
    heads = lambda n: pl.BlockSpec((n, tm, LANES), lambda i: (0, i, 0))
    return pl.pallas_call(
        _mid_kernel,
        grid=(s // tm,),
        in_specs=[
            pl.BlockSpec((tm, D_MODEL), lambda i: (i, 0)),
            heads(MOBA_HEADS), heads(DIFF_HEADS), heads(DSA_HEADS),
            whole((D_MODEL, D_MODEL)), whole((1, D_MODEL)), whole((D_MODEL, CROSS_W)),
            whole((1, LANES)), whole((n_mem, CROSS_W)), whole((n_mem, CROSS_W)),
            whole((CROSS_W, D_MODEL)),
        ],
        out_specs=pl.BlockSpec((tm, D_MODEL), lambda i: (i, 0)),
        out_shape=jax.ShapeDtypeStruct((s, D_MODEL), F32),
        compiler_params=_cparams(("arbitrary",)),
        name="out_proj_cross",
    )(x, om, od, osa, wout, gc, wq, gq, ck, cv, wo)


FFN_HALO = 16
FFN_ROW_CHUNK = 256

def _ffn_kernel(x_ref, xp_ref, g_ref, wg_ref, wv_ref, cwg_ref, cwv_ref, cbg_ref, cbv_ref,
                wd_ref, o_ref, h_scr, *, tm):
    i = pl.program_id(0)
    f = pl.program_id(1)

    @pl.when(f == 0)
    def _():
        prev = _rms_rows(xp_ref[...], g_ref[...])
        h_scr[0:FFN_HALO, :] = jnp.where(i > 0, prev, 0.0).astype(BF16)
        h_scr[FFN_HALO:, :] = _rms_rows(x_ref[...], g_ref[...]).astype(BF16)
        o_ref[...] = x_ref[...]

    for r0 in range(0, tm, FFN_ROW_CHUNK):
        h = h_scr[r0:r0 + FFN_ROW_CHUNK + FFN_HALO, :]

        def conv(w_ref, cw_ref, cb_ref):
            u = jnp.dot(h, w_ref[...], preferred_element_type=F32)
            cw = cw_ref[...]
            uc = (cw[0:1] * pltpu.roll(u, 2, 0) + cw[1:2] * pltpu.roll(u, 1, 0) + cw[2:3] * u
                  + cb_ref[...])
            return uc[FFN_HALO:, :]

        gate = conv(wg_ref, cwg_ref, cbg_ref)
        val = conv(wv_ref, cwv_ref, cbv_ref)
        act = (gate * jax.nn.sigmoid(gate) * val).astype(BF16)
        o_ref[r0:r0 + FFN_ROW_CHUNK, :] += jnp.dot(act, wd_ref[...], preferred_element_type=F32)


def _ffn(x, g, w_up_p, cw_p, cb_p, w_down_p, s):
    tm = min(s, 512)
    nf = D_FF_PAD // FF_TILE
    halo_blocks = tm // FFN_HALO
    return pl.pallas_call(
        functools.partial(_ffn_kernel, tm=tm),
        grid=(s // tm, nf),
        in_specs=[
            pl.BlockSpec((tm, D_MODEL), lambda i, f: (i, 0)),
            pl.BlockSpec((FFN_HALO, D_MODEL), lambda i, f: (jnp.maximum(i * halo_blocks - 1, 0), 0)),
            pl.BlockSpec((1, D_MODEL), lambda i, f: (0, 0)),
            pl.BlockSpec((D_MODEL, FF_TILE), lambda i, f: (0, f)),
            pl.BlockSpec((D_MODEL, FF_TILE), lambda i, f: (0, f + nf)),
            pl.BlockSpec((3, FF_TILE), lambda i, f: (0, f)),
            pl.BlockSpec((3, FF_TILE), lambda i, f: (0, f + nf)),
            pl.BlockSpec((1, FF_TILE), lambda i, f: (0, f)),
            pl.BlockSpec((1, FF_TILE), lambda i, f: (0, f + nf)),
            pl.BlockSpec((FF_TILE, D_MODEL), lambda i, f: (f, 0)),
        ],
        out_specs=pl.BlockSpec((tm, D_MODEL), lambda i, f: (i, 0)),
        out_shape=jax.ShapeDtypeStruct((s, D_MODEL), F32),
        scratch_shapes=[pltpu.VMEM((tm + FFN_HALO, D_MODEL), BF16)],
        compiler_params=_cparams(("arbitrary", "arbitrary")),
        name="ffn",
    )(x, x, g, w_up_p, w_up_p, cw_p, cw_p, cb_p, cb_p, w_down_p)


CAST_ROWS = 128


def _cast_kernel(w_ref, o_ref, *, rows_in):
    live = pl.program_id(0) * CAST_ROWS < rows_in
    o_ref[...] = jnp.where(live, w_ref[0], 0.0).astype(BF16)


def _cast_layer(w, layer, rows_out=None):
    _, rows, n = w.shape
    rows_out = rows if rows_out is None else rows_out
    last = rows // CAST_ROWS - 1
    return pl.pallas_call(
        functools.partial(_cast_kernel, rows_in=rows),
        grid=(rows_out // CAST_ROWS,),
        in_specs=[pl.BlockSpec((1, CAST_ROWS, n), lambda i: (layer, jnp.minimum(i, last), 0))],
        out_specs=pl.BlockSpec((CAST_ROWS, n), lambda i: (i, 0)),
        out_shape=jax.ShapeDtypeStruct((rows_out, n), BF16),
        compiler_params=_cparams(("arbitrary",)),
        name="cast_weight",
    )(w)


def _cast_up_kernel(w_ref, o_ref):
    w = w_ref[0]
    pad = jnp.zeros((w.shape[0], D_FF_PAD - D_FF), BF16)
    o_ref[:, 0:D_FF] = w[:, 0:D_FF].astype(BF16)
    o_ref[:, D_FF:D_FF_PAD] = pad
    o_ref[:, D_FF_PAD:D_FF_PAD + D_FF] = w[:, D_FF:].astype(BF16)
    o_ref[:, D_FF_PAD + D_FF:] = pad


def _cast_up(w_up, layer):
    return pl.pallas_call(
        _cast_up_kernel,
        grid=(D_MODEL // CAST_ROWS,),
        in_specs=[pl.BlockSpec((1, CAST_ROWS, 2 * D_FF), lambda i: (layer, i, 0))],
        out_specs=pl.BlockSpec((CAST_ROWS, 2 * D_FF_PAD), lambda i: (i, 0)),
        out_shape=jax.ShapeDtypeStruct((D_MODEL, 2 * D_FF_PAD), BF16),
        compiler_params=_cparams(("arbitrary",)),
        name="cast_w_up",
    )(w_up)


def _split_pad(a):
    r = a.shape[0]
    halves = jnp.pad(a.reshape(r, 2, D_FF), ((0, 0), (0, 0), (0, D_FF_PAD - D_FF)))
    return halves.reshape(r, 2 * D_FF_PAD)


def kernel(x, mem, positions, attn_norm, w_in, moba_qk_gain, diff_qk_gain, diff_lambda,
           diff_subln, dsa_qk_gain, w_out, cross_norm, mem_norm, cross_wq, cross_wkv,
           cross_qk_gain, cross_wo, ffn_norm, ffn_w_up, ffn_conv_w, ffn_conv_b, ffn_w_down):
    b, s, _ = x.shape
    assert b == 1 and s % DSA_TQ == 0 and s % MOBA_BLOCK == 0
    xs = x.reshape(s, D_MODEL)
    mem2d = mem.reshape(mem.shape[1], D_MODEL)
    tabs = _rope_tables(positions, s)
    row = lambda v: v.reshape(1, -1)
    for l in range(DEPTH):
        gains = jnp.concatenate([
            moba_qk_gain[l], dsa_qk_gain[l], jnp.tile(diff_qk_gain[l], (1, 2)),
            jnp.zeros((2, LANES), F32)], axis=0)
        main, tail = _in_proj(xs, row(attn_norm[l]), w_in, l, tabs, gains, s)
        o_moba = _moba(main, s)
        o_diff = _diff(main, diff_lambda[l], row(diff_subln[l]), l, s)
        o_dsa = _dsa(main, tail, s)
        ck, cv = _memkv(mem2d, row(mem_norm[l]), _cast_layer(cross_wkv, l), row(cross_qk_gain[l, 1]))
        xs = _mid(xs, o_moba, o_diff, o_dsa, _cast_layer(w_out, l), row(cross_norm[l]),
                  _cast_layer(cross_wq, l), row(cross_qk_gain[l, 0]), ck, cv,
                  _cast_layer(cross_wo, l), s)
        w_up_p = _cast_up(ffn_w_up, l)
        cw_p = _split_pad(ffn_conv_w[l])
        cb_p = _split_pad(row(ffn_conv_b[l]))
        w_down_p = _cast_layer(ffn_w_down, l, D_FF_PAD)
        xs = _ffn(xs, row(ffn_norm[l]), w_up_p, cw_p, cb_p, w_down_p, s)
    return xs.reshape(b, s, D_MODEL)
```

```python
import functools
import math

import jax
import jax.numpy as jnp
from jax import lax
from jax.experimental import pallas as pl
from jax.experimental.pallas import tpu as pltpu

F32 = jnp.float32
BF16 = jnp.bfloat16
I32 = jnp.int32

D_MODEL = 2048
DEPTH = 2
HEAD_DIM = 128
MOBA_HEADS = 6
DIFF_HEADS = 4
DSA_HEADS = 6
MOBA_BLOCK = 256
MOBA_TOPK = 3
DIFF_QK_DIM = 64
DSA_TOPK = 256
IDX_HEADS = 8
IDX_DIM = 64
MEM_LEN = 256
CROSS_HEADS = 4
CROSS_W = CROSS_HEADS * HEAD_DIM
D_FF = 5504
ROPE_THETA = 10000.0
RMS_EPS = 1e-6

LANES = 128
PROJ_TN = 256
PROJ_ROW_CHUNK = 128
D_IN = 6728
D_IN_PAD = 6912
D_IN_TAIL = D_IN - (D_IN_PAD - 256)
N_PROJ_TILES = D_IN_PAD // PROJ_TN
N_HEAD_BLOCKS = D_IN_PAD // LANES
FF_TILE = 512
D_FF_PAD = 5632
NEG = -1e30
LOG2E = 1.4426950408889634
FLASH_WIDTH = 4
MOBA_HEADS_PER_STEP = 6
DSA_HEADS_PER_STEP = 6
DIFF_HEADS_PER_STEP = 2
INT_MIN = -(2 ** 31)
INT_MAX = 2 ** 31 - 1
SEARCH_LINEAR_STEPS = 40
VMEM_LIMIT = 56 * 1024 * 1024

BLK_MQ, BLK_MK, BLK_MV = 0, 6, 12
BLK_DQ, BLK_DK, BLK_DV = 18, 22, 26
BLK_SQ, BLK_SK, BLK_SV = 30, 36, 42
BLK_IQ, BLK_IK = 48, 52


def _cparams(sem):
    return pltpu.CompilerParams(dimension_semantics=sem, vmem_limit_bytes=VMEM_LIMIT)


def _rms_rows(x, g):
    return x * lax.rsqrt(jnp.mean(x * x, axis=-1, keepdims=True) + RMS_EPS) * g


def _dot_t(a, b):
    return lax.dot_general(a, b, (((1,), (1,)), ((), ())), preferred_element_type=F32)


def _rope_tab_kernel(pos_ref, inv128_ref, inv64_ref, c128_ref, s128_ref, c64_ref, s64_ref):
    p = pos_ref[...].astype(F32)
    lane = lax.broadcasted_iota(I32, (1, LANES), 1)
    a = p * inv128_ref[...]
    sa = jnp.sin(a)
    c128_ref[...] = jnp.cos(a)
    s128_ref[...] = jnp.where(lane < 64, -sa, sa)
    b = p * inv64_ref[...]
    sb = jnp.sin(b)
    c64_ref[...] = jnp.cos(b)
    s64_ref[...] = jnp.where((lane % 64) < 32, -sb, sb)


def _rope_tables(positions, s):
    tm = min(s, 1024)
    inv_a = ROPE_THETA ** (-(jnp.arange(64, dtype=F32) * 2.0 / 128))
    inv_b = ROPE_THETA ** (-(jnp.arange(32, dtype=F32) * 2.0 / 64))
    inv128 = jnp.tile(inv_a, 2).reshape(1, LANES)
    inv64 = jnp.tile(inv_b, 4).reshape(1, LANES)
    pos = positions.reshape(s, 1)
    tab = jax.ShapeDtypeStruct((s, LANES), F32)
    row = pl.BlockSpec((tm, LANES), lambda i: (i, 0))
    cst = pl.BlockSpec((1, LANES), lambda i: (0, 0))
    return pl.pallas_call(
        _rope_tab_kernel,
        grid=(s // tm,),
        in_specs=[pl.BlockSpec((tm, 1), lambda i: (i, 0)), cst, cst],
        out_specs=[row, row, row, row],
        out_shape=[tab, tab, tab, tab],
        compiler_params=_cparams(("arbitrary",)),
        name="rope_tables",
    )(pos, inv128, inv64)


def _proj_kernel(x_ref, g_ref, w_ref, c128_ref, s128_ref, c64_ref, s64_ref, gains_ref,
                 main_ref, tail_ref, h_scr):
    j = pl.program_id(1)

    @pl.when(j == 0)
    def _():
        h_scr[...] = _rms_rows(x_ref[...], g_ref[...]).astype(BF16)

    lane = lax.broadcasted_iota(I32, (1, LANES), 1)
    lo = lane < 64
    tm = h_scr.shape[0]
    chunks = [slice(r0, r0 + PROJ_ROW_CHUNK) for r0 in range(0, tm, PROJ_ROW_CHUNK)]

    def product(rs, w):
        return jnp.dot(h_scr[rs, :], w, preferred_element_type=F32)

    def rope128(y, rs):
        return y * c128_ref[rs, :] + pltpu.roll(y, 64, 1) * s128_ref[rs, :]

    def rope64(y, rs):
        partner = jnp.where((lane % 64) < 32, pltpu.roll(y, 96, 1), pltpu.roll(y, 32, 1))
        return y * c64_ref[rs, :] + partner * s64_ref[rs, :]

    def norm64(y, g):
        y2 = y * y
        s_lo = jnp.sum(jnp.where(lo, y2, 0.0), axis=-1, keepdims=True)
        s_hi = jnp.sum(jnp.where(lo, 0.0, y2), axis=-1, keepdims=True)
        ms = jnp.where(lo, s_lo, s_hi) * (1.0 / 64)
        return y * lax.rsqrt(ms + RMS_EPS) * g

    def emit(fn):
        w = w_ref[0].astype(BF16)
        for rs in chunks:
            r = product(rs, w)
            for c in range(2):
                main_ref[c, rs, :] = fn(r[:, c * LANES:(c + 1) * LANES], rs).astype(BF16)

    def qk128(row, post=None):
        fn = lambda y, rs: rope128(_rms_rows(y, gains_ref[row:row + 1, :]), rs)
        return fn if post is None else (lambda y, rs: fn(y, rs) * post)

    def qk64(row, post=None):
        fn = lambda y, rs: rope64(norm64(y, gains_ref[row:row + 1, :]), rs)
        return fn if post is None else (lambda y, rs: fn(y, rs) * post)

    ident = lambda y, rs: y
    q128 = HEAD_DIM ** -0.5 * LOG2E
    q64 = DIFF_QK_DIM ** -0.5 * LOG2E
    segments = (
        (0, 3, qk128(0, q128)), (3, 6, qk128(1)), (6, 9, ident),
        (9, 11, qk64(4, q64)), (11, 13, qk64(5)), (13, 15, ident),
        (15, 18, qk128(2, q128)), (18, 21, qk128(3)), (21, 24, ident),
        (24, 26, rope64),
    )
    for first, last, fn in segments:
        pl.when((j >= first) & (j < last))(functools.partial(emit, fn))

    @pl.when(j == N_PROJ_TILES - 1)
    def _():
        w = w_ref[0, :, 0:LANES].astype(BF16)
        for rs in chunks:
            y = jnp.where(lane < D_IN_TAIL, product(rs, w), 0.0)
            kk = jnp.where(lo, rope64(y, rs), 0.0)
            main_ref[0, rs, :] = kk.astype(BF16)
            main_ref[1, rs, :] = pltpu.roll(kk, 64, 1).astype(BF16)
            tail_ref[rs, :] = (y * (IDX_HEADS ** -0.5)) * (IDX_DIM ** -0.5)


def _in_proj(x, g, w_in, layer, tabs, gains, s):
    tm = min(s, 1024)
    c128, s128, c64, s64 = tabs
    row = pl.BlockSpec((tm, LANES), lambda i, j: (i, 0))
    return pl.pallas_call(
        _proj_kernel,
        grid=(s // tm, N_PROJ_TILES),
        in_specs=[
            pl.BlockSpec((tm, D_MODEL), lambda i, j: (i, 0)),
            pl.BlockSpec((1, D_MODEL), lambda i, j: (0, 0)),
            pl.BlockSpec((1, D_MODEL, PROJ_TN), lambda i, j: (layer, 0, j)),
            row, row, row, row,
            pl.BlockSpec((8, LANES), lambda i, j: (0, 0)),
        ],
        out_specs=[
            pl.BlockSpec((2, tm, LANES), lambda i, j: (j, i, 0)),
            pl.BlockSpec((tm, LANES), lambda i, j: (i, 0)),
        ],
        out_shape=[
            jax.ShapeDtypeStruct((N_HEAD_BLOCKS, s, LANES), BF16),
            jax.ShapeDtypeStruct((s, LANES), F32),
        ],
        scratch_shapes=[pltpu.VMEM((tm, D_MODEL), BF16)],
        compiler_params=_cparams(("arbitrary", "arbitrary")),
        name="in_proj",
    )(x, g, w_in, c128, s128, c64, s64, gains)


def _flash_slab(q, k_slab, v_slab, masks, carry, bias=None):
    m, l, acc = carry
    sc = _dot_t(q, k_slab)
    if bias is not None:
        sc = sc + bias
    if masks is not None:
        tk = sc.shape[1] // len(masks)
        sc = jnp.concatenate([jnp.where(mk, sc[:, u * tk:(u + 1) * tk], NEG)
                              for u, mk in enumerate(masks)], axis=1)
    m_new = jnp.maximum(m, jnp.max(sc, axis=-1, keepdims=True))
    p = jnp.exp2(sc - m_new)
    alpha = jnp.exp2(m - m_new)
    l = alpha * l + jnp.sum(p, axis=-1, keepdims=True)
    acc = alpha * acc + jnp.dot(p.astype(BF16), v_slab, preferred_element_type=F32)
    return m_new, l, acc


def _flash_init(rows):
    return (jnp.full((rows, 1), NEG, F32), jnp.zeros((rows, 1), F32),
            jnp.zeros((rows, HEAD_DIM), F32))


def _flash_loop(streams, first_slab, n_slabs, states, shared_masks=None, shared_bias=None):
    def slab(g, sts):
        common = None if shared_masks is None else shared_masks(g)
        bias = None if shared_bias is None else shared_bias(g)
        out = []
        for (q, load_slab), st in zip(streams, sts):
            k_slab, v_slab, masks = load_slab(g)
            out.append(_flash_slab(q, k_slab, v_slab, masks if common is None else common, st, bias))
        return tuple(out)
    return lax.fori_loop(first_slab, n_slabs, slab, tuple(states))


def _flash_finish(state):
    _, l, acc = state
    return acc / l


def _moba_kernel(q_ref, k_ref, v_ref, o_ref, kbar_scr, *, nb, n_sel):
    blk = MOBA_BLOCK
    i = pl.program_id(1)

    @pl.when(i == 0)
    def _():
        kbar_scr[...] = jnp.zeros_like(kbar_scr)
        for hh in range(MOBA_HEADS_PER_STEP):
            def mean_block(n, c, hh=hh):
                kk = k_ref[hh, pl.ds(pl.multiple_of(n * blk, blk), blk), :].astype(F32)
                kbar_scr[hh, pl.ds(n, 1), :] = jnp.sum(kk, axis=0, keepdims=True) * (1.0 / blk)
                return c
            lax.fori_loop(0, nb, mean_block, 0)

    blk_id = lax.broadcasted_iota(I32, (LANES, blk), 0)
    valid = blk_id < i
    bl = lax.broadcasted_iota(I32, (blk, LANES), 1)

    def stream(hh):
        q = q_ref[hh]
        gate = _dot_t(kbar_scr[hh].astype(BF16), q)
        g = jnp.where(valid, gate, -jnp.inf)
        sel_t = jnp.zeros((LANES, blk), F32)
        for _ in range(n_sel):
            mx = jnp.max(g, axis=0, keepdims=True)
            first = jnp.min(jnp.where(g == mx, blk_id, LANES), axis=0, keepdims=True)
            hit = blk_id == first
            sel_t = jnp.where(hit, 1.0, sel_t)
            g = jnp.where(hit, -jnp.inf, g)
        sel = jnp.where(valid, sel_t, 0.0).T

        lim_tab = jnp.where(bl == i, row, jnp.where(sel > 0.0, blk - 1.0, -1.0))

        def load(g):
            masks = []
            for u in range(FLASH_WIDTH):
                j = g * FLASH_WIDTH + u
                lim = jnp.max(jnp.where(bl == j, lim_tab, -1.0), axis=-1, keepdims=True)
                masks.append(cols <= lim)
            sl = pl.ds(pl.multiple_of(g * slab_keys, slab_keys), slab_keys)
            return k_ref[hh, sl, :], v_ref[hh, sl, :], masks
        return q, load

    row = lax.broadcasted_iota(I32, (blk, LANES), 0).astype(F32)
    cols = lax.broadcasted_iota(I32, (blk, blk), 1).astype(F32)
    slab_keys = FLASH_WIDTH * blk
    streams = [stream(hh) for hh in range(MOBA_HEADS_PER_STEP)]
    n_slabs = (i + FLASH_WIDTH) // FLASH_WIDTH
    states = _flash_loop(streams, 0, n_slabs, [_flash_init(blk) for _ in streams])
    for hh, st in enumerate(states):
        o_ref[hh] = _flash_finish(st).astype(BF16)


def _moba(main, s):
    blk = MOBA_BLOCK
    nb = s // blk
    hps = MOBA_HEADS_PER_STEP
    assert nb <= LANES and nb % FLASH_WIDTH == 0 and MOBA_HEADS % hps == 0
    assert BLK_MQ % hps == 0 and BLK_MK % hps == 0 and BLK_MV % hps == 0
    n_sel = max(1, min(MOBA_TOPK, nb - 1))
    return pl.pallas_call(
        functools.partial(_moba_kernel, nb=nb, n_sel=n_sel),
        grid=(MOBA_HEADS // hps, nb),
        in_specs=[
            pl.BlockSpec((hps, blk, LANES), lambda h, i: (BLK_MQ // hps + h, i, 0)),
            pl.BlockSpec((hps, s, LANES), lambda h, i: (BLK_MK // hps + h, 0, 0),
                         pipeline_mode=pl.Buffered(1)),
            pl.BlockSpec((hps, s, LANES), lambda h, i: (BLK_MV // hps + h, 0, 0),
                         pipeline_mode=pl.Buffered(1)),
        ],
        out_specs=pl.BlockSpec((hps, blk, LANES), lambda h, i: (h, i, 0)),
        out_shape=jax.ShapeDtypeStruct((MOBA_HEADS, s, LANES), BF16),
        scratch_shapes=[pltpu.VMEM((hps, LANES, LANES), F32)],
        compiler_params=_cparams(("arbitrary", "arbitrary")),
        name="moba",
    )(main, main, main)


def _diff_kernel(q_ref, k_ref, v_ref, lam_ref, g_ref, o_ref, *, tq, lam_init):
    i = pl.program_id(1)
    lane = lax.broadcasted_iota(I32, (tq, LANES), 1)
    slab_keys = FLASH_WIDTH * tq
    row = lax.broadcasted_iota(I32, (2 * tq, 1), 0) & (tq - 1)
    cols = lax.broadcasted_iota(I32, (2 * tq, tq), 1)

    def tail_masks(g):
        masks = []
        for u in range(FLASH_WIDTH):
            j = g * FLASH_WIDTH + u
            masks.append(cols <= jnp.where(j < i, tq - 1, jnp.where(j == i, row, -1)))
        return masks

    def stream(hh):
        q = q_ref[hh]
        zero = jnp.zeros_like(q)
        qq = jnp.concatenate([jnp.where(lane < 64, q, zero), jnp.where(lane < 64, zero, q)], axis=0)

        def load(g):
            sl = pl.ds(pl.multiple_of(g * slab_keys, slab_keys), slab_keys)
            return k_ref[hh, sl, :], v_ref[hh, sl, :], None
        return qq, load

    streams = [stream(hh) for hh in range(DIFF_HEADS_PER_STEP)]
    full = i // FLASH_WIDTH
    states = _flash_loop(streams, 0, full, [_flash_init(2 * tq) for _ in streams])
    states = _flash_loop(streams, full, full + 1, states, shared_masks=tail_masks)
    lf = lam_ref[...]
    lam = (jnp.exp(jnp.sum(lf[0:1] * lf[1:2], axis=-1, keepdims=True))
           - jnp.exp(jnp.sum(lf[2:3] * lf[3:4], axis=-1, keepdims=True)) + lam_init)
    for hh, st in enumerate(states):
        o = _flash_finish(st)
        a = o[:tq] - lam * o[tq:]
        o_ref[hh] = (_rms_rows(a, g_ref[...]) * (1.0 - lam_init)).astype(BF16)


def _diff(main, lam_params, subln, layer, s):
    tq = min(s, 256)
    hps = DIFF_HEADS_PER_STEP
    assert (s // tq) % FLASH_WIDTH == 0 and DIFF_HEADS % hps == 0
    assert BLK_DQ % hps == 0 and BLK_DK % hps == 0 and BLK_DV % hps == 0
    lam_init = 0.8 - 0.6 * math.exp(-0.3 * layer)
    return pl.pallas_call(
        functools.partial(_diff_kernel, tq=tq, lam_init=lam_init),
        grid=(DIFF_HEADS // hps, s // tq),
        in_specs=[
            pl.BlockSpec((hps, tq, LANES), lambda h, i: (BLK_DQ // hps + h, i, 0)),
            pl.BlockSpec((hps, s, LANES), lambda h, i: (BLK_DK // hps + h, 0, 0),
                         pipeline_mode=pl.Buffered(1)),
            pl.BlockSpec((hps, s, LANES), lambda h, i: (BLK_DV // hps + h, 0, 0),
                         pipeline_mode=pl.Buffered(1)),
            pl.BlockSpec((4, DIFF_QK_DIM), lambda h, i: (0, 0)),
            pl.BlockSpec((1, LANES), lambda h, i: (0, 0)),
        ],
        out_specs=pl.BlockSpec((hps, tq, LANES), lambda h, i: (h, i, 0)),
        out_shape=jax.ShapeDtypeStruct((DIFF_HEADS, s, LANES), BF16),
        compiler_params=_cparams(("arbitrary", "arbitrary")),
        name="diff_attn",
    )(main, main, main, lam_params, subln)


DSA_TQ = 256


def _sort_key(sc):
    sc = jnp.where(sc == 0.0, 0.0, sc)
    b = lax.bitcast_convert_type(sc, I32)
    return b ^ ((b >> 31) & 0x7FFFFFFF)


def _dsa_kernel(qi_ref, ki_ref, w_ref, q_ref, k_ref, v_ref, o_ref,
                key_scr, keyt_scr, wb_scr, thr_scr, *, n_keep):
    t = DSA_TQ
    i = pl.program_id(0)
    w = w_ref[...]
    for h in range(IDX_HEADS):
        wb_scr[h] = jnp.broadcast_to(w[:, 64 + h:65 + h], (t, t))

    def tile(j):
        return pl.ds(pl.multiple_of(j * t, t), t)

    q_idx = qi_ref[...].reshape(IDX_HEADS // 2 * t, LANES)

    def index_scores(j):
        s_even = _dot_t(q_idx, ki_ref[0, tile(j), :])
        s_odd = _dot_t(q_idx, ki_ref[1, tile(j), :])
        sc = jnp.zeros((t, t), F32)
        for b in range(IDX_HEADS // 2):
            rs = slice(b * t, (b + 1) * t)
            sc = sc + wb_scr[2 * b] * jnp.maximum(s_even[rs], 0.0)
            sc = sc + wb_scr[2 * b + 1] * jnp.maximum(s_odd[rs], 0.0)
        return sc

    rows = lax.broadcasted_iota(I32, (t, t), 0)
    cols = lax.broadcasted_iota(I32, (t, t), 1)
    causal = cols <= rows

    def put_keys(j, keys):
        key_scr[:, tile(j)] = keys
        keyt_scr[tile(j), :] = keys.T

    def fill_past(j, c):
        put_keys(j, _sort_key(index_scores(j)))
        return c

    def fill_pair(jj, c):
        return fill_past(2 * jj + 1, fill_past(2 * jj, c))
    lax.fori_loop(0, i // 2, fill_pair, 0)
    lax.fori_loop(2 * (i // 2), i, fill_past, 0)
    put_keys(i, jnp.where(causal, _sort_key(index_scores(i)), INT_MIN))
    n_steps = (i + FLASH_WIDTH) // FLASH_WIDTH

    def fill_pad(j, c):
        key_scr[:, tile(j)] = jnp.full((t, t), INT_MIN, I32)
        return c
    lax.fori_loop(i + 1, n_steps * FLASH_WIDTH, fill_pad, 0)

    def count_keys(preds):
        def step(j, cnts):
            kk = keyt_scr[tile(j), :].reshape(t // 8, 8, t)
            return tuple(c + jnp.sum(jnp.where(p(kk), 1, 0), axis=0) for c, p in zip(cnts, preds))

        def pair(jj, cnts):
            return step(2 * jj + 1, step(2 * jj, cnts))
        zero = jnp.zeros((8, t), I32)
        n_pairs = (i + 1) // 2
        cnts = lax.fori_loop(0, n_pairs, pair, tuple(zero for _ in preds))
        cnts = lax.fori_loop(2 * n_pairs, i + 1, step, cnts)
        return [jnp.sum(c, axis=0, keepdims=True) for c in cnts]

    def ge(c):
        return lambda kk: kk >= c

    def extremes(j, c, own):
        kk = keyt_scr[tile(j), :].reshape(t // 8, 8, t)
        floor = jnp.where(kk == INT_MIN, INT_MAX, kk) if own else kk
        return jnp.minimum(c[0], jnp.min(floor, axis=0)), jnp.maximum(c[1], jnp.max(kk, axis=0))
    mn, mx = lax.fori_loop(0, i, functools.partial(extremes, own=False),
                           (jnp.full((8, t), INT_MAX, I32), jnp.full((8, t), INT_MIN, I32)))
    mn, mx = extremes(i, (mn, mx), own=True)
    lo = jnp.broadcast_to(jnp.min(mn, axis=0, keepdims=True), (8, t))
    hi = jnp.broadcast_to(jnp.max(mx, axis=0, keepdims=True), (8, t)) + 1
    zero_key = jnp.zeros((8, t), I32)
    n_ge0, n_gt0 = count_keys([ge(zero_key), ge(zero_key + 1)])
    n_adm = i * t + lax.broadcasted_iota(I32, (1, t), 1) + 1
    few = n_adm < n_keep
    at_zero = (n_gt0 < n_keep) & (n_ge0 >= n_keep)
    above = n_gt0 >= n_keep
    lo_cnt = jnp.broadcast_to(jnp.where(above, n_gt0, jnp.where(at_zero, n_ge0, n_adm)), (8, t))
    lo = jnp.where(above, jnp.maximum(lo, 1), jnp.where(at_zero, 0, lo))
    hi = jnp.where(above, hi, jnp.where(at_zero, 1, jnp.minimum(hi, 0)))

    def finished(lo, hi, lo_cnt):
        return jnp.where(few | (lo_cnt == n_keep) | (hi - lo == 1), 1, 0)

    def key_to_score(k):
        return lax.bitcast_convert_type(k ^ ((k >> 31) & 0x7FFFFFFF), F32)

    def narrow(carry):
        it, _, lo, hi, lo_cnt = carry
        all_done = jnp.min(finished(lo, hi, lo_cnt))
        mid = 0.5 * key_to_score(lo) + 0.5 * key_to_score(hi)
        b = lax.bitcast_convert_type(mid, I32)
        by_score = b ^ ((b >> 31) & 0x7FFFFFFF)
        by_key = (lo >> 1) + (hi >> 1) + (lo & hi & 1)
        cand = jnp.where(it < SEARCH_LINEAR_STEPS, by_score, by_key)
        cand = jnp.minimum(jnp.maximum(cand, lo + 1), hi - 1)
        cnt, = count_keys([ge(cand)])
        live = finished(lo, hi, lo_cnt) == 0
        up = live & (cnt >= n_keep)
        down = live & (cnt < n_keep)
        lo, lo_cnt, hi = jnp.where(up, cand, lo), jnp.where(up, cnt, lo_cnt), jnp.where(down, cand, hi)
        return it + 1, all_done, lo, hi, lo_cnt

    first = (jnp.int32(0), jnp.int32(0), lo, hi, lo_cnt)
    _, _, lo, hi, lo_cnt = lax.while_loop(lambda c: c[1] == 0, narrow, first)
    thr_t = jnp.where(few, INT_MIN + 1, lo)
    thr_scr[...] = jnp.broadcast_to(thr_t[0:1], (t, t)).T

    surplus = jnp.max(jnp.where(few, 0, jnp.where(lo_cnt > n_keep, 1, 0)))

    @pl.when(surplus > 0)
    def _():
        n_gt_t, = count_keys([lambda kk: kk > thr_t])
        need_t = (n_keep - n_gt_t).astype(F32)
        need = jnp.broadcast_to(need_t, (t, t)).T[:, 0:1]
        thr = thr_scr[...]
        upper = jnp.where(rows <= cols, 1.0, 0.0).astype(BF16)

        def drop_late_ties(j, before):
            kk = key_scr[:, tile(j)]
            tie = kk == thr
            seen = before + jnp.dot(jnp.where(tie, 1.0, 0.0).astype(BF16), upper,
                                    preferred_element_type=F32)
            key_scr[:, tile(j)] = jnp.where(tie, jnp.where(seen > need, INT_MIN, kk), kk)
            return seen[:, t - 1:t]
        lax.fori_loop(0, i + 1, drop_late_ties, jnp.zeros((t, 1), F32))

    slab_keys = FLASH_WIDTH * t

    def key_to_bias(j, c):
        keep = key_scr[:, tile(j)] >= thr_scr[...]
        key_scr[:, tile(j)] = lax.bitcast_convert_type(jnp.where(keep, 0.0, NEG), I32)
        return c
    lax.fori_loop(0, n_steps * FLASH_WIDTH, key_to_bias, 0)

    def admitted_bias(g):
        sl = pl.ds(pl.multiple_of(g * slab_keys, slab_keys), slab_keys)
        return lax.bitcast_convert_type(key_scr[:, sl], F32)

    def head_group(hg, c):
        def stream(h):
            def load(g):
                sl = pl.ds(pl.multiple_of(g * slab_keys, slab_keys), slab_keys)
                return k_ref[h, sl, :], v_ref[h, sl, :], None
            return q_ref[h], load

        heads = [hg * DSA_HEADS_PER_STEP + u for u in range(DSA_HEADS_PER_STEP)]
        states = _flash_loop([stream(h) for h in heads], 0, n_steps, [_flash_init(t) for _ in heads],
                             shared_bias=admitted_bias)
        for h, st in zip(heads, states):
            o_ref[h] = _flash_finish(st).astype(BF16)
        return c
    lax.fori_loop(0, DSA_HEADS // DSA_HEADS_PER_STEP, head_group, 0)


def _dsa(main, tail, s):
    t = DSA_TQ
    n_keep = min(DSA_TOPK, s // 4)
    slab = FLASH_WIDTH * t
    s_pad = -(-s // slab) * slab
    resident = dict(pipeline_mode=pl.Buffered(1))
    return pl.pallas_call(
        functools.partial(_dsa_kernel, n_keep=n_keep),
        grid=(s // t,),
        in_specs=[
            pl.BlockSpec((IDX_HEADS // 2, t, LANES), lambda i: (BLK_IQ // 4, i, 0)),
            pl.BlockSpec((2, s, LANES), lambda i: (BLK_IK // 2, 0, 0), **resident),
            pl.BlockSpec((t, LANES), lambda i: (i, 0)),
            pl.BlockSpec((DSA_HEADS, t, LANES), lambda i: (BLK_SQ // 6, i, 0)),
            pl.BlockSpec((DSA_HEADS, s, LANES), lambda i: (BLK_SK // 6, 0, 0), **resident),
            pl.BlockSpec((DSA_HEADS, s, LANES), lambda i: (BLK_SV // 6, 0, 0), **resident),
        ],
        out_specs=pl.BlockSpec((DSA_HEADS, t, LANES), lambda i: (0, i, 0)),
        out_shape=jax.ShapeDtypeStruct((DSA_HEADS, s, LANES), BF16),
        scratch_shapes=[
            pltpu.VMEM((t, s_pad), I32),
            pltpu.VMEM((s, t), I32),
            pltpu.VMEM((IDX_HEADS, t, t), F32),
            pltpu.VMEM((t, t), I32),
        ],
        compiler_params=_cparams(("arbitrary",)),
        name="dsa",
    )(main, main, tail, main, main, main)


def _memkv_kernel(mem_ref, g_ref, wkv_ref, gk_ref, ck_ref, cv_ref):
    m = _rms_rows(mem_ref[...], g_ref[...]).astype(BF16)
    kv = jnp.dot(m, wkv_ref[...], preferred_element_type=F32)
    for h in range(CROSS_HEADS):
        sl = slice(h * HEAD_DIM, (h + 1) * HEAD_DIM)
        ck_ref[:, sl] = _rms_rows(kv[:, sl], gk_ref[...]).astype(BF16)
    cv_ref[...] = kv[:, CROSS_W:].astype(BF16)


def _memkv(mem2d, g, wkv, gk):
    n = mem2d.shape[0]
    out = jax.ShapeDtypeStruct((n, CROSS_W), BF16)
    return pl.pallas_call(
        _memkv_kernel,
        out_shape=[out, out],
        compiler_params=pltpu.CompilerParams(vmem_limit_bytes=VMEM_LIMIT),
        name="mem_kv",
    )(mem2d, g, wkv, gk)


def _mid_kernel(x_ref, om_ref, od_ref, os_ref, wout_ref, gc_ref, wq_ref, gq_ref,
                ck_ref, cv_ref, wo_ref, o_ref):
    heads = ([om_ref[h] for h in range(MOBA_HEADS)] + [od_ref[h] for h in range(DIFF_HEADS)]
             + [os_ref[h] for h in range(DSA_HEADS)])
    mixed = jnp.concatenate(heads, axis=-1)
    x1 = x_ref[...] + jnp.dot(mixed, wout_ref[...], preferred_element_type=F32)
    hq = _rms_rows(x1, gc_ref[...]).astype(BF16)
    cq = jnp.dot(hq, wq_ref[...], preferred_element_type=F32)
    scale = HEAD_DIM ** -0.5
    outs = []
    for h in range(CROSS_HEADS):
        sl = slice(h * HEAD_DIM, (h + 1) * HEAD_DIM)
        qh = _rms_rows(cq[:, sl], gq_ref[...]).astype(BF16)
        sc = _dot_t(qh, ck_ref[:, sl]) * scale
        m = jnp.max(sc, axis=-1, keepdims=True)
        p = jnp.exp(sc - m)
        l = jnp.sum(p, axis=-1, keepdims=True)
        outs.append(jnp.dot(p.astype(BF16), cv_ref[:, sl], preferred_element_type=F32) / l)
    co = jnp.concatenate(outs, axis=-1).astype(BF16)
    o_ref[...] = x1 + jnp.dot(co, wo_ref[...], preferred_element_type=F32)


def _mid(x, om, od, osa, wout, gc, wq, gq, ck, cv, wo, s):
    tm = min(s, 512)
    n_mem = ck.shape[0]
    whole = lambda shape: pl.BlockSpec(shape, lambda i: (0,) * len(shape), pipeline_mode=pl.Buffered(1))
    heads = lambda n: pl.BlockSpec((n, tm, LANES), lambda i: (0, i, 0))
    return pl.pallas_call(
        _mid_kernel,
        grid=(s // tm,),
        in_specs=[
            pl.BlockSpec((tm, D_MODEL), lambda i: (i, 0)),
            heads(MOBA_HEADS), heads(DIFF_HEADS), heads(DSA_HEADS),
            whole((D_MODEL, D_MODEL)), whole((1, D_MODEL)), whole((D_MODEL, CROSS_W)),
            whole((1, LANES)), whole((n_mem, CROSS_W)), whole((n_mem, CROSS_W)),
            whole((CROSS_W, D_MODEL)),
        ],
        out_specs=pl.BlockSpec((tm, D_MODEL), lambda i: (i, 0)),
        out_shape=jax.ShapeDtypeStruct((s, D_MODEL), F32),
        compiler_params=_cparams(("arbitrary",)),
        name="out_proj_cross",
    )(x, om, od, osa, wout, gc, wq, gq, ck, cv, wo)


FFN_HALO = 16
FFN_ROW_CHUNK = 256

def _ffn_kernel(x_ref, xp_ref, g_ref, wg_ref, wv_ref, cwg_ref, cwv_ref, cbg_ref, cbv_ref,
                wd_ref, o_ref, h_scr, *, tm):
    i = pl.program_id(0)
    f = pl.program_id(1)

    @pl.when(f == 0)
    def _():
        prev = _rms_rows(xp_ref[...], g_ref[...])
        h_scr[0:FFN_HALO, :] = jnp.where(i > 0, prev, 0.0).astype(BF16)
        h_scr[FFN_HALO:, :] = _rms_rows(x_ref[...], g_ref[...]).astype(BF16)
        o_ref[...] = x_ref[...]

    for r0 in range(0, tm, FFN_ROW_CHUNK):
        h = h_scr[r0:r0 + FFN_ROW_CHUNK + FFN_HALO, :]

        def conv(w_ref, cw_ref, cb_ref):
            u = jnp.dot(h, w_ref[...], preferred_element_type=F32)
            cw = cw_ref[...]
            uc = (cw[0:1] * pltpu.roll(u, 2, 0) + cw[1:2] * pltpu.roll(u, 1, 0) + cw[2:3] * u
                  + cb_ref[...])
            return uc[FFN_HALO:, :]

        gate = conv(wg_ref, cwg_ref, cbg_ref)
        val = conv(wv_ref, cwv_ref, cbv_ref)
        act = (gate * jax.nn.sigmoid(gate) * val).astype(BF16)
        o_ref[r0:r0 + FFN_ROW_CHUNK, :] += jnp.dot(act, wd_ref[...], preferred_element_type=F32)


def _ffn(x, g, w_up_p, cw_p, cb_p, w_down_p, s):
    tm = min(s, 512)
    nf = D_FF_PAD // FF_TILE
    halo_blocks = tm // FFN_HALO
    return pl.pallas_call(
        functools.partial(_ffn_kernel, tm=tm),
        grid=(s // tm, nf),
        in_specs=[
            pl.BlockSpec((tm, D_MODEL), lambda i, f: (i, 0)),
            pl.BlockSpec((FFN_HALO, D_MODEL), lambda i, f: (jnp.maximum(i * halo_blocks - 1, 0), 0)),
            pl.BlockSpec((1, D_MODEL), lambda i, f: (0, 0)),
            pl.BlockSpec((D_MODEL, FF_TILE), lambda i, f: (0, f)),
            pl.BlockSpec((D_MODEL, FF_TILE), lambda i, f: (0, f + nf)),
            pl.BlockSpec((3, FF_TILE), lambda i, f: (0, f)),
            pl.BlockSpec((3, FF_TILE), lambda i, f: (0, f + nf)),
            pl.BlockSpec((1, FF_TILE), lambda i, f: (0, f)),
            pl.BlockSpec((1, FF_TILE), lambda i, f: (0, f + nf)),
            pl.BlockSpec((FF_TILE, D_MODEL), lambda i, f: (f, 0)),
        ],
        out_specs=pl.BlockSpec((tm, D_MODEL), lambda i, f: (i, 0)),
        out_shape=jax.ShapeDtypeStruct((s, D_MODEL), F32),
        scratch_shapes=[pltpu.VMEM((tm + FFN_HALO, D_MODEL), BF16)],
        compiler_params=_cparams(("arbitrary", "arbitrary")),
        name="ffn",
    )(x, x, g, w_up_p, w_up_p, cw_p, cw_p, cb_p, cb_p, w_down_p)


CAST_ROWS = 128


def _cast_kernel(w_ref, o_ref, *, rows_in):
    live = pl.program_id(0) * CAST_ROWS < rows_in
    o_ref[...] = jnp.where(live, w_ref[0], 0.0).astype(BF16)


def _cast_layer(w, layer, rows_out=None):
    _, rows, n = w.shape
    rows_out = rows if rows_out is None else rows_out
    last = rows // CAST_ROWS - 1
    return pl.pallas_call(
        functools.partial(_cast_kernel, rows_in=rows),
        grid=(rows_out // CAST_ROWS,),
        in_specs=[pl.BlockSpec((1, CAST_ROWS, n), lambda i: (layer, jnp.minimum(i, last), 0))],
        out_specs=pl.BlockSpec((CAST_ROWS, n), lambda i: (i, 0)),
        out_shape=jax.ShapeDtypeStruct((rows_out, n), BF16),
        compiler_params=_cparams(("arbitrary",)),
        name="cast_weight",
    )(w)


def _cast_up_kernel(w_ref, o_ref):
    w = w_ref[0]
    pad = jnp.zeros((w.shape[0], D_FF_PAD - D_FF), BF16)
    o_ref[:, 0:D_FF] = w[:, 0:D_FF].astype(BF16)
    o_ref[:, D_FF:D_FF_PAD] = pad
    o_ref[:, D_FF_PAD:D_FF_PAD + D_FF] = w[:, D_FF:].astype(BF16)
    o_ref[:, D_FF_PAD + D_FF:] = pad


def _cast_up(w_up, layer):
    return pl.pallas_call(
        _cast_up_kernel,
        grid=(D_MODEL // CAST_ROWS,),
        in_specs=[pl.BlockSpec((1, CAST_ROWS, 2 * D_FF), lambda i: (layer, i, 0))],
        out_specs=pl.BlockSpec((CAST_ROWS, 2 * D_FF_PAD), lambda i: (i, 0)),
        out_shape=jax.ShapeDtypeStruct((D_MODEL, 2 * D_FF_PAD), BF16),
        compiler_params=_cparams(("arbitrary",)),
        name="cast_w_up",
    )(w_up)


def _split_pad(a):
    r = a.shape[0]
    halves = jnp.pad(a.reshape(r, 2, D_FF), ((0, 0), (0, 0), (0, D_FF_PAD - D_FF)))
    return halves.reshape(r, 2 * D_FF_PAD)


def kernel(x, mem, positions, attn_norm, w_in, moba_qk_gain, diff_qk_gain, diff_lambda,
           diff_subln, dsa_qk_gain, w_out, cross_norm, mem_norm, cross_wq, cross_wkv,
           cross_qk_gain, cross_wo, ffn_norm, ffn_w_up, ffn_conv_w, ffn_conv_b, ffn_w_down):
    b, s, _ = x.shape
    assert b == 1 and s % DSA_TQ == 0 and s % MOBA_BLOCK == 0
    xs = x.reshape(s, D_MODEL)
    mem2d = mem.reshape(mem.shape[1], D_MODEL)
    tabs = _rope_tables(positions, s)
    row = lambda v: v.reshape(1, -1)
    for l in range(DEPTH):
        gains = jnp.concatenate([
            moba_qk_gain[l], dsa_qk_gain[l], jnp.tile(diff_qk_gain[l], (1, 2)),
            jnp.zeros((2, LANES), F32)], axis=0)
        main, tail = _in_proj(xs, row(attn_norm[l]), w_in, l, tabs, gains, s)
        o_moba = _moba(main, s)
        o_diff = _diff(main, diff_lambda[l], row(diff_subln[l]), l, s)
        o_dsa = _dsa(main, tail, s)
        ck, cv = _memkv(mem2d, row(mem_norm[l]), _cast_layer(cross_wkv, l), row(cross_qk_gain[l, 1]))
        xs = _mid(xs, o_moba, o_diff, o_dsa, _cast_layer(w_out, l), row(cross_norm[l]),
                  _cast_layer(cross_wq, l), row(cross_qk_gain[l, 0]), ck, cv,
                  _cast_layer(cross_wo, l), s)
        w_up_p = _cast_up(ffn_w_up, l)
        cw_p = _split_pad(ffn_conv_w[l])
        cb_p = _split_pad(row(ffn_conv_b[l]))
        w_down_p = _cast_layer(ffn_w_down, l, D_FF_PAD)
        xs = _ffn(xs, row(ffn_norm[l]), w_up_p, cw_p, cb_p, w_down_p, s)
    return xs.reshape(b, s, D_MODEL)
```

```python
import functools
import math

import jax
import jax.numpy as jnp
from jax import lax
from jax.experimental import pallas as pl
from jax.experimental.pallas import tpu as pltpu

F32 = jnp.float32
BF16 = jnp.bfloat16
I32 = jnp.int32

D_MODEL = 2048
DEPTH = 2
HEAD_DIM = 128
MOBA_HEADS = 6
DIFF_HEADS = 4
DSA_HEADS = 6
MOBA_BLOCK = 256
MOBA_TOPK = 3
DIFF_QK_DIM = 64
DSA_TOPK = 256
IDX_HEADS = 8
IDX_DIM = 64
MEM_LEN = 256
CROSS_HEADS = 4
CROSS_W = CROSS_HEADS * HEAD_DIM
D_FF = 5504
ROPE_THETA = 10000.0
RMS_EPS = 1e-6

LANES = 128
PROJ_TN = 256
PROJ_ROW_CHUNK = 128
D_IN = 6728
D_IN_PAD = 6912
D_IN_TAIL = D_IN - (D_IN_PAD - 256)
N_PROJ_TILES = D_IN_PAD // PROJ_TN
N_HEAD_BLOCKS = D_IN_PAD // LANES
FF_TILE = 512
D_FF_PAD = 5632
NEG = -1e30
LOG2E = 1.4426950408889634
FLASH_WIDTH = 4
MOBA_HEADS_PER_STEP = 6
DSA_HEADS_PER_STEP = 6
DIFF_HEADS_PER_STEP = 2
INT_MIN = -(2 ** 31)
INT_MAX = 2 ** 31 - 1
SEARCH_LINEAR_STEPS = 40
SEARCH_PASSES_PER_TEST = 2
VMEM_LIMIT = 56 * 1024 * 1024

BLK_MQ, BLK_MK, BLK_MV = 0, 6, 12
BLK_DQ, BLK_DK, BLK_DV = 18, 22, 26
BLK_SQ, BLK_SK, BLK_SV = 30, 36, 42
BLK_IQ, BLK_IK = 48, 52


def _cparams(sem):
    return pltpu.CompilerParams(dimension_semantics=sem, vmem_limit_bytes=VMEM_LIMIT)


def _rms_rows(x, g):
    return x * lax.rsqrt(jnp.mean(x * x, axis=-1, keepdims=True) + RMS_EPS) * g


def _dot_t(a, b):
    return lax.dot_general(a, b, (((1,), (1,)), ((), ())), preferred_element_type=F32)


def _rope_tab_kernel(pos_ref, inv128_ref, inv64_ref, c128_ref, s128_ref, c64_ref, s64_ref):
    p = pos_ref[...].astype(F32)
    lane = lax.broadcasted_iota(I32, (1, LANES), 1)
    a = p * inv128_ref[...]
    sa = jnp.sin(a)
    c128_ref[...] = jnp.cos(a)
    s128_ref[...] = jnp.where(lane < 64, -sa, sa)
    b = p * inv64_ref[...]
    sb = jnp.sin(b)
    c64_ref[...] = jnp.cos(b)
    s64_ref[...] = jnp.where((lane % 64) < 32, -sb, sb)


def _rope_tables(positions, s):
    tm = min(s, 1024)
    inv_a = ROPE_THETA ** (-(jnp.arange(64, dtype=F32) * 2.0 / 128))
    inv_b = ROPE_THETA ** (-(jnp.arange(32, dtype=F32) * 2.0 / 64))
    inv128 = jnp.tile(inv_a, 2).reshape(1, LANES)
    inv64 = jnp.tile(inv_b, 4).reshape(1, LANES)
    pos = positions.reshape(s, 1)
    tab = jax.ShapeDtypeStruct((s, LANES), F32)
    row = pl.BlockSpec((tm, LANES), lambda i: (i, 0))
    cst = pl.BlockSpec((1, LANES), lambda i: (0, 0))
    return pl.pallas_call(
        _rope_tab_kernel,
        grid=(s // tm,),
        in_specs=[pl.BlockSpec((tm, 1), lambda i: (i, 0)), cst, cst],
        out_specs=[row, row, row, row],
        out_shape=[tab, tab, tab, tab],
        compiler_params=_cparams(("arbitrary",)),
        name="rope_tables",
    )(pos, inv128, inv64)


def _proj_kernel(x_ref, g_ref, w_ref, c128_ref, s128_ref, c64_ref, s64_ref, gains_ref,
                 main_ref, tail_ref, h_scr):
    j = pl.program_id(1)

    @pl.when(j == 0)
    def _():
        h_scr[...] = _rms_rows(x_ref[...], g_ref[...]).astype(BF16)

    lane = lax.broadcasted_iota(I32, (1, LANES), 1)
    lo = lane < 64
    tm = h_scr.shape[0]
    chunks = [slice(r0, r0 + PROJ_ROW_CHUNK) for r0 in range(0, tm, PROJ_ROW_CHUNK)]

    def product(rs, w):
        return jnp.dot(h_scr[rs, :], w, preferred_element_type=F32)

    def rope128(y, rs):
        return y * c128_ref[rs, :] + pltpu.roll(y, 64, 1) * s128_ref[rs, :]

    def rope64(y, rs):
        partner = jnp.where((lane % 64) < 32, pltpu.roll(y, 96, 1), pltpu.roll(y, 32, 1))
        return y * c64_ref[rs, :] + partner * s64_ref[rs, :]

    def norm64(y, g):
        y2 = y * y
        s_lo = jnp.sum(jnp.where(lo, y2, 0.0), axis=-1, keepdims=True)
        s_hi = jnp.sum(jnp.where(lo, 0.0, y2), axis=-1, keepdims=True)
        ms = jnp.where(lo, s_lo, s_hi) * (1.0 / 64)
        return y * lax.rsqrt(ms + RMS_EPS) * g

    def emit(fn):
        w = w_ref[0].astype(BF16)
        for rs in chunks:
            r = product(rs, w)
            for c in range(2):
                main_ref[c, rs, :] = fn(r[:, c * LANES:(c + 1) * LANES], rs).astype(BF16)

    def qk128(row, post=None):
        fn = lambda y, rs: rope128(_rms_rows(y, gains_ref[row:row + 1, :]), rs)
        return fn if post is None else (lambda y, rs: fn(y, rs) * post)

    def qk64(row, post=None):
        fn = lambda y, rs: rope64(norm64(y, gains_ref[row:row + 1, :]), rs)
        return fn if post is None else (lambda y, rs: fn(y, rs) * post)

    ident = lambda y, rs: y
    q128 = HEAD_DIM ** -0.5 * LOG2E
    q64 = DIFF_QK_DIM ** -0.5 * LOG2E
    segments = (
        (0, 3, qk128(0, q128)), (3, 6, qk128(1)), (6, 9, ident),
        (9, 11, qk64(4, q64)), (11, 13, qk64(5)), (13, 15, ident),
        (15, 18, qk128(2, q128)), (18, 21, qk128(3)), (21, 24, ident),
        (24, 26, rope64),
    )
    for first, last, fn in segments:
        pl.when((j >= first) & (j < last))(functools.partial(emit, fn))

    @pl.when(j == N_PROJ_TILES - 1)
    def _():
        w = w_ref[0, :, 0:LANES].astype(BF16)
        for rs in chunks:
            y = jnp.where(lane < D_IN_TAIL, product(rs, w), 0.0)
            kk = jnp.where(lo, rope64(y, rs), 0.0)
            main_ref[0, rs, :] = kk.astype(BF16)
            main_ref[1, rs, :] = pltpu.roll(kk, 64, 1).astype(BF16)
            tail_ref[rs, :] = (y * (IDX_HEADS ** -0.5)) * (IDX_DIM ** -0.5)


def _in_proj(x, g, w_in, layer, tabs, gains, s):
    tm = min(s, 1024)
    c128, s128, c64, s64 = tabs
    row = pl.BlockSpec((tm, LANES), lambda i, j: (i, 0))
    return pl.pallas_call(
        _proj_kernel,
        grid=(s // tm, N_PROJ_TILES),
        in_specs=[
            pl.BlockSpec((tm, D_MODEL), lambda i, j: (i, 0)),
            pl.BlockSpec((1, D_MODEL), lambda i, j: (0, 0)),
            pl.BlockSpec((1, D_MODEL, PROJ_TN), lambda i, j: (layer, 0, j)),
            row, row, row, row,
            pl.BlockSpec((8, LANES), lambda i, j: (0, 0)),
        ],
        out_specs=[
            pl.BlockSpec((2, tm, LANES), lambda i, j: (j, i, 0)),
            pl.BlockSpec((tm, LANES), lambda i, j: (i, 0)),
        ],
        out_shape=[
            jax.ShapeDtypeStruct((N_HEAD_BLOCKS, s, LANES), BF16),
            jax.ShapeDtypeStruct((s, LANES), F32),
        ],
        scratch_shapes=[pltpu.VMEM((tm, D_MODEL), BF16)],
        compiler_params=_cparams(("arbitrary", "arbitrary")),
        name="in_proj",
    )(x, g, w_in, c128, s128, c64, s64, gains)


def _flash_slab(q, k_slab, v_slab, masks, carry, bias=None):
    m, l, acc = carry
    sc = _dot_t(q, k_slab)
    if bias is not None:
        sc = sc + bias
    if masks is not None:
        tk = sc.shape[1] // len(masks)
        sc = jnp.concatenate([jnp.where(mk, sc[:, u * tk:(u + 1) * tk], NEG)
                              for u, mk in enumerate(masks)], axis=1)
    m_new = jnp.maximum(m, jnp.max(sc, axis=-1, keepdims=True))
    p = jnp.exp2(sc - m_new)
    alpha = jnp.exp2(m - m_new)
    l = alpha * l + jnp.sum(p, axis=-1, keepdims=True)
    acc = alpha * acc + jnp.dot(p.astype(BF16), v_slab, preferred_element_type=F32)
    return m_new, l, acc


def _flash_init(rows):
    return (jnp.full((rows, 1), NEG, F32), jnp.zeros((rows, 1), F32),
            jnp.zeros((rows, HEAD_DIM), F32))


def _flash_loop(streams, first_slab, n_slabs, states, shared_masks=None, shared_bias=None):
    def slab(g, sts):
        common = None if shared_masks is None else shared_masks(g)
        bias = None if shared_bias is None else shared_bias(g)
        out = []
        for (q, load_slab), st in zip(streams, sts):
            k_slab, v_slab, masks = load_slab(g)
            out.append(_flash_slab(q, k_slab, v_slab, masks if common is None else common, st, bias))
        return tuple(out)
    return lax.fori_loop(first_slab, n_slabs, slab, tuple(states))


def _flash_finish(state):
    _, l, acc = state
    return acc / l


def _moba_kernel(q_ref, k_ref, v_ref, o_ref, kbar_scr, *, nb, n_sel):
    blk = MOBA_BLOCK
    i = pl.program_id(1)

    @pl.when(i == 0)
    def _():
        kbar_scr[...] = jnp.zeros_like(kbar_scr)
        for hh in range(MOBA_HEADS_PER_STEP):
            def mean_block(n, c, hh=hh):
                kk = k_ref[hh, pl.ds(pl.multiple_of(n * blk, blk), blk), :].astype(F32)
                kbar_scr[hh, pl.ds(n, 1), :] = jnp.sum(kk, axis=0, keepdims=True) * (1.0 / blk)
                return c
            lax.fori_loop(0, nb, mean_block, 0)

    blk_id = lax.broadcasted_iota(I32, (LANES, blk), 0)
    valid = blk_id < i
    bl = lax.broadcasted_iota(I32, (blk, LANES), 1)

    def stream(hh):
        q = q_ref[hh]
        gate = _dot_t(kbar_scr[hh].astype(BF16), q)
        g = jnp.where(valid, gate, -jnp.inf)
        sel_t = jnp.zeros((LANES, blk), F32)
        for _ in range(n_sel):
            mx = jnp.max(g, axis=0, keepdims=True)
            first = jnp.min(jnp.where(g == mx, blk_id, LANES), axis=0, keepdims=True)
            hit = blk_id == first
            sel_t = jnp.where(hit, 1.0, sel_t)
            g = jnp.where(hit, -jnp.inf, g)
        sel = jnp.where(valid, sel_t, 0.0).T

        lim_tab = jnp.where(bl == i, row, jnp.where(sel > 0.0, blk - 1.0, -1.0))

        def load(g):
            masks = []
            for u in range(FLASH_WIDTH):
                j = g * FLASH_WIDTH + u
                lim = jnp.max(jnp.where(bl == j, lim_tab, -1.0), axis=-1, keepdims=True)
                masks.append(cols <= lim)
            sl = pl.ds(pl.multiple_of(g * slab_keys, slab_keys), slab_keys)
            return k_ref[hh, sl, :], v_ref[hh, sl, :], masks
        return q, load

    row = lax.broadcasted_iota(I32, (blk, LANES), 0).astype(F32)
    cols = lax.broadcasted_iota(I32, (blk, blk), 1).astype(F32)
    slab_keys = FLASH_WIDTH * blk
    streams = [stream(hh) for hh in range(MOBA_HEADS_PER_STEP)]
    n_slabs = (i + FLASH_WIDTH) // FLASH_WIDTH
    states = _flash_loop(streams, 0, n_slabs, [_flash_init(blk) for _ in streams])
    for hh, st in enumerate(states):
        o_ref[hh] = _flash_finish(st).astype(BF16)


def _moba(main, s):
    blk = MOBA_BLOCK
    nb = s // blk
    hps = MOBA_HEADS_PER_STEP
    assert nb <= LANES and nb % FLASH_WIDTH == 0 and MOBA_HEADS % hps == 0
    assert BLK_MQ % hps == 0 and BLK_MK % hps == 0 and BLK_MV % hps == 0
    n_sel = max(1, min(MOBA_TOPK, nb - 1))
    return pl.pallas_call(
        functools.partial(_moba_kernel, nb=nb, n_sel=n_sel),
        grid=(MOBA_HEADS // hps, nb),
        in_specs=[
            pl.BlockSpec((hps, blk, LANES), lambda h, i: (BLK_MQ // hps + h, i, 0)),
            pl.BlockSpec((hps, s, LANES), lambda h, i: (BLK_MK // hps + h, 0, 0),
                         pipeline_mode=pl.Buffered(1)),
            pl.BlockSpec((hps, s, LANES), lambda h, i: (BLK_MV // hps + h, 0, 0),
                         pipeline_mode=pl.Buffered(1)),
        ],
        out_specs=pl.BlockSpec((hps, blk, LANES), lambda h, i: (h, i, 0)),
        out_shape=jax.ShapeDtypeStruct((MOBA_HEADS, s, LANES), BF16),
        scratch_shapes=[pltpu.VMEM((hps, LANES, LANES), F32)],
        compiler_params=_cparams(("arbitrary", "arbitrary")),
        name="moba",
    )(main, main, main)


def _diff_kernel(q_ref, k_ref, v_ref, lam_ref, g_ref, o_ref, *, tq, lam_init):
    i = pl.program_id(1)
    lane = lax.broadcasted_iota(I32, (tq, LANES), 1)
    slab_keys = FLASH_WIDTH * tq
    row = lax.broadcasted_iota(I32, (2 * tq, 1), 0) & (tq - 1)
    cols = lax.broadcasted_iota(I32, (2 * tq, tq), 1)

    def tail_masks(g):
        masks = []
        for u in range(FLASH_WIDTH):
            j = g * FLASH_WIDTH + u
            masks.append(cols <= jnp.where(j < i, tq - 1, jnp.where(j == i, row, -1)))
        return masks

    def stream(hh):
        q = q_ref[hh]
        zero = jnp.zeros_like(q)
        qq = jnp.concatenate([jnp.where(lane < 64, q, zero), jnp.where(lane < 64, zero, q)], axis=0)

        def load(g):
            sl = pl.ds(pl.multiple_of(g * slab_keys, slab_keys), slab_keys)
            return k_ref[hh, sl, :], v_ref[hh, sl, :], None
        return qq, load

    streams = [stream(hh) for hh in range(DIFF_HEADS_PER_STEP)]
    full = i // FLASH_WIDTH
    states = _flash_loop(streams, 0, full, [_flash_init(2 * tq) for _ in streams])
    states = _flash_loop(streams, full, full + 1, states, shared_masks=tail_masks)
    lf = lam_ref[...]
    lam = (jnp.exp(jnp.sum(lf[0:1] * lf[1:2], axis=-1, keepdims=True))
           - jnp.exp(jnp.sum(lf[2:3] * lf[3:4], axis=-1, keepdims=True)) + lam_init)
    for hh, st in enumerate(states):
        o = _flash_finish(st)
        a = o[:tq] - lam * o[tq:]
        o_ref[hh] = (_rms_rows(a, g_ref[...]) * (1.0 - lam_init)).astype(BF16)


def _diff(main, lam_params, subln, layer, s):
    tq = min(s, 256)
    hps = DIFF_HEADS_PER_STEP
    assert (s // tq) % FLASH_WIDTH == 0 and DIFF_HEADS % hps == 0
    assert BLK_DQ % hps == 0 and BLK_DK % hps == 0 and BLK_DV % hps == 0
    lam_init = 0.8 - 0.6 * math.exp(-0.3 * layer)
    return pl.pallas_call(
        functools.partial(_diff_kernel, tq=tq, lam_init=lam_init),
        grid=(DIFF_HEADS // hps, s // tq),
        in_specs=[
            pl.BlockSpec((hps, tq, LANES), lambda h, i: (BLK_DQ // hps + h, i, 0)),
            pl.BlockSpec((hps, s, LANES), lambda h, i: (BLK_DK // hps + h, 0, 0)),
            pl.BlockSpec((hps, s, LANES), lambda h, i: (BLK_DV // hps + h, 0, 0)),
            pl.BlockSpec((4, DIFF_QK_DIM), lambda h, i: (0, 0)),
            pl.BlockSpec((1, LANES), lambda h, i: (0, 0)),
        ],
        out_specs=pl.BlockSpec((hps, tq, LANES), lambda h, i: (h, i, 0)),
        out_shape=jax.ShapeDtypeStruct((DIFF_HEADS, s, LANES), BF16),
        compiler_params=_cparams(("arbitrary", "arbitrary")),
        name="diff_attn",
    )(main, main, main, lam_params, subln)


DSA_TQ = 256


def _sort_key(sc):
    sc = jnp.where(sc == 0.0, 0.0, sc)
    b = lax.bitcast_convert_type(sc, I32)
    return b ^ ((b >> 31) & 0x7FFFFFFF)


def _dsa_kernel(qi_ref, ki_ref, w_ref, q_ref, k_ref, v_ref, o_ref,
                key_scr, keyt_scr, wb_scr, thr_scr, *, n_keep):
    t = DSA_TQ
    i = pl.program_id(0)
    w = w_ref[...]
    for h in range(IDX_HEADS):
        wb_scr[h] = jnp.broadcast_to(w[:, 64 + h:65 + h], (t, t))

    def tile(j):
        return pl.ds(pl.multiple_of(j * t, t), t)

    q_idx = qi_ref[...].reshape(IDX_HEADS // 2 * t, LANES)

    def index_scores(j):
        s_even = _dot_t(q_idx, ki_ref[0, tile(j), :])
        s_odd = _dot_t(q_idx, ki_ref[1, tile(j), :])
        sc = jnp.zeros((t, t), F32)
        for b in range(IDX_HEADS // 2):
            rs = slice(b * t, (b + 1) * t)
            sc = sc + wb_scr[2 * b] * jnp.maximum(s_even[rs], 0.0)
            sc = sc + wb_scr[2 * b + 1] * jnp.maximum(s_odd[rs], 0.0)
        return sc

    rows = lax.broadcasted_iota(I32, (t, t), 0)
    cols = lax.broadcasted_iota(I32, (t, t), 1)
    causal = cols <= rows

    def put_keys(j, keys):
        key_scr[:, tile(j)] = keys
        keyt_scr[tile(j), :] = keys.T

    def fill_past(j, c):
        put_keys(j, _sort_key(index_scores(j)))
        return c

    def fill_pair(jj, c):
        return fill_past(2 * jj + 1, fill_past(2 * jj, c))
    lax.fori_loop(0, i // 2, fill_pair, 0)
    lax.fori_loop(2 * (i // 2), i, fill_past, 0)
    put_keys(i, jnp.where(causal, _sort_key(index_scores(i)), INT_MIN))
    n_steps = (i + FLASH_WIDTH) // FLASH_WIDTH

    def fill_pad(j, c):
        key_scr[:, tile(j)] = jnp.full((t, t), INT_MIN, I32)
        return c
    lax.fori_loop(i + 1, n_steps * FLASH_WIDTH, fill_pad, 0)

    def count_keys(preds):
        def step(j, cnts):
            kk = keyt_scr[tile(j), :].reshape(t // 8, 8, t)
            return tuple(c + jnp.sum(jnp.where(p(kk), 1, 0), axis=0) for c, p in zip(cnts, preds))

        def pair(jj, cnts):
            return step(2 * jj + 1, step(2 * jj, cnts))
        zero = jnp.zeros((8, t), I32)
        n_pairs = (i + 1) // 2
        cnts = lax.fori_loop(0, n_pairs, pair, tuple(zero for _ in preds))
        cnts = lax.fori_loop(2 * n_pairs, i + 1, step, cnts)
        return [jnp.sum(c, axis=0, keepdims=True) for c in cnts]

    def ge(c):
        return lambda kk: kk >= c

    def extremes(j, c, own):
        kk = keyt_scr[tile(j), :].reshape(t // 8, 8, t)
        floor = jnp.where(kk == INT_MIN, INT_MAX, kk) if own else kk
        return jnp.minimum(c[0], jnp.min(floor, axis=0)), jnp.maximum(c[1], jnp.max(kk, axis=0))
    mn, mx = lax.fori_loop(0, i, functools.partial(extremes, own=False),
                           (jnp.full((8, t), INT_MAX, I32), jnp.full((8, t), INT_MIN, I32)))
    mn, mx = extremes(i, (mn, mx), own=True)
    lo = jnp.broadcast_to(jnp.min(mn, axis=0, keepdims=True), (8, t))
    hi = jnp.broadcast_to(jnp.max(mx, axis=0, keepdims=True), (8, t)) + 1
    zero_key = jnp.zeros((8, t), I32)
    n_ge0, n_gt0 = count_keys([ge(zero_key), ge(zero_key + 1)])
    n_adm = i * t + lax.broadcasted_iota(I32, (1, t), 1) + 1
    few = n_adm < n_keep
    at_zero = (n_gt0 < n_keep) & (n_ge0 >= n_keep)
    above = n_gt0 >= n_keep
    lo_cnt = jnp.broadcast_to(jnp.where(above, n_gt0, jnp.where(at_zero, n_ge0, n_adm)), (8, t))
    lo = jnp.where(above, jnp.maximum(lo, 1), jnp.where(at_zero, 0, lo))
    hi = jnp.where(above, hi, jnp.where(at_zero, 1, jnp.minimum(hi, 0)))

    def finished(lo, hi, lo_cnt):
        return jnp.where(few | (lo_cnt == n_keep) | (hi - lo == 1), 1, 0)

    def key_to_score(k):
        return lax.bitcast_convert_type(k ^ ((k >> 31) & 0x7FFFFFFF), F32)

    def narrow(carry):
        it, _, lo, hi, lo_cnt = carry
        all_done = jnp.min(finished(lo, hi, lo_cnt))
        for _ in range(SEARCH_PASSES_PER_TEST):
            it, lo, hi, lo_cnt = one_pass(it, lo, hi, lo_cnt)
        return it, all_done, lo, hi, lo_cnt

    def one_pass(it, lo, hi, lo_cnt):
        mid = 0.5 * key_to_score(lo) + 0.5 * key_to_score(hi)
        b = lax.bitcast_convert_type(mid, I32)
        by_score = b ^ ((b >> 31) & 0x7FFFFFFF)
        by_key = (lo >> 1) + (hi >> 1) + (lo & hi & 1)
        cand = jnp.where(it < SEARCH_LINEAR_STEPS, by_score, by_key)
        cand = jnp.minimum(jnp.maximum(cand, lo + 1), hi - 1)
        cnt, = count_keys([ge(cand)])
        live = finished(lo, hi, lo_cnt) == 0
        up = live & (cnt >= n_keep)
        down = live & (cnt < n_keep)
        lo, lo_cnt, hi = jnp.where(up, cand, lo), jnp.where(up, cnt, lo_cnt), jnp.where(down, cand, hi)
        return it + 1, lo, hi, lo_cnt

    first = (jnp.int32(0), jnp.int32(0), lo, hi, lo_cnt)
    _, _, lo, hi, lo_cnt = lax.while_loop(lambda c: c[1] == 0, narrow, first)
    thr_t = jnp.where(few, INT_MIN + 1, lo)
    thr_scr[...] = jnp.broadcast_to(thr_t[0:1], (t, t)).T

    surplus = jnp.max(jnp.where(few, 0, jnp.where(lo_cnt > n_keep, 1, 0)))

    @pl.when(surplus > 0)
    def _():
        n_gt_t, = count_keys([lambda kk: kk > thr_t])
        need_t = (n_keep - n_gt_t).astype(F32)
        need = jnp.broadcast_to(need_t, (t, t)).T[:, 0:1]
        thr = thr_scr[...]
        upper = jnp.where(rows <= cols, 1.0, 0.0).astype(BF16)

        def drop_late_ties(j, before):
            kk = key_scr[:, tile(j)]
            tie = kk == thr
            seen = before + jnp.dot(jnp.where(tie, 1.0, 0.0).astype(BF16), upper,
                                    preferred_element_type=F32)
            key_scr[:, tile(j)] = jnp.where(tie, jnp.where(seen > need, INT_MIN, kk), kk)
            return seen[:, t - 1:t]
        lax.fori_loop(0, i + 1, drop_late_ties, jnp.zeros((t, 1), F32))

    slab_keys = FLASH_WIDTH * t

    def key_to_bias(j, c):
        keep = key_scr[:, tile(j)] >= thr_scr[...]
        key_scr[:, tile(j)] = lax.bitcast_convert_type(jnp.where(keep, 0.0, NEG), I32)
        return c
    lax.fori_loop(0, n_steps * FLASH_WIDTH, key_to_bias, 0)

    def admitted_bias(g):
        sl = pl.ds(pl.multiple_of(g * slab_keys, slab_keys), slab_keys)
        return lax.bitcast_convert_type(key_scr[:, sl], F32)

    def head_group(hg, c):
        def stream(h):
            def load(g):
                sl = pl.ds(pl.multiple_of(g * slab_keys, slab_keys), slab_keys)
                return k_ref[h, sl, :], v_ref[h, sl, :], None
            return q_ref[h], load

        heads = [hg * DSA_HEADS_PER_STEP + u for u in range(DSA_HEADS_PER_STEP)]
        states = _flash_loop([stream(h) for h in heads], 0, n_steps, [_flash_init(t) for _ in heads],
                             shared_bias=admitted_bias)
        for h, st in zip(heads, states):
            o_ref[h] = _flash_finish(st).astype(BF16)
        return c
    lax.fori_loop(0, DSA_HEADS // DSA_HEADS_PER_STEP, head_group, 0)


def _dsa(main, tail, s):
    t = DSA_TQ
    n_keep = min(DSA_TOPK, s // 4)
    slab = FLASH_WIDTH * t
    s_pad = -(-s // slab) * slab
    resident = dict(pipeline_mode=pl.Buffered(1))
    return pl.pallas_call(
        functools.partial(_dsa_kernel, n_keep=n_keep),
        grid=(s // t,),
        in_specs=[
            pl.BlockSpec((IDX_HEADS // 2, t, LANES), lambda i: (BLK_IQ // 4, i, 0)),
            pl.BlockSpec((2, s, LANES), lambda i: (BLK_IK // 2, 0, 0), **resident),
            pl.BlockSpec((t, LANES), lambda i: (i, 0)),
            pl.BlockSpec((DSA_HEADS, t, LANES), lambda i: (BLK_SQ // 6, i, 0)),
            pl.BlockSpec((DSA_HEADS, s, LANES), lambda i: (BLK_SK // 6, 0, 0), **resident),
            pl.BlockSpec((DSA_HEADS, s, LANES), lambda i: (BLK_SV // 6, 0, 0), **resident),
        ],
        out_specs=pl.BlockSpec((DSA_HEADS, t, LANES), lambda i: (0, i, 0)),
        out_shape=jax.ShapeDtypeStruct((DSA_HEADS, s, LANES), BF16),
        scratch_shapes=[
            pltpu.VMEM((t, s_pad), I32),
            pltpu.VMEM((s, t), I32),
            pltpu.VMEM((IDX_HEADS, t, t), F32),
            pltpu.VMEM((t, t), I32),
        ],
        compiler_params=_cparams(("arbitrary",)),
        name="dsa",
    )(main, main, tail, main, main, main)


def _memkv_kernel(mem_ref, g_ref, wkv_ref, gk_ref, ck_ref, cv_ref):
    m = _rms_rows(mem_ref[...], g_ref[...]).astype(BF16)
    kv = jnp.dot(m, wkv_ref[...], preferred_element_type=F32)
    for h in range(CROSS_HEADS):
        sl = slice(h * HEAD_DIM, (h + 1) * HEAD_DIM)
        ck_ref[:, sl] = _rms_rows(kv[:, sl], gk_ref[...]).astype(BF16)
    cv_ref[...] = kv[:, CROSS_W:].astype(BF16)


def _memkv(mem2d, g, wkv, gk):
    n = mem2d.shape[0]
    out = jax.ShapeDtypeStruct((n, CROSS_W), BF16)
    return pl.pallas_call(
        _memkv_kernel,
        out_shape=[out, out],
        compiler_params=pltpu.CompilerParams(vmem_limit_bytes=VMEM_LIMIT),
        name="mem_kv",
    )(mem2d, g, wkv, gk)


def _mid_kernel(x_ref, om_ref, od_ref, os_ref, wout_ref, gc_ref, wq_ref, gq_ref,
                ck_ref, cv_ref, wo_ref, o_ref):
    heads = ([om_ref[h] for h in range(MOBA_HEADS)] + [od_ref[h] for h in range(DIFF_HEADS)]
             + [os_ref[h] for h in range(DSA_HEADS)])
    mixed = jnp.concatenate(heads, axis=-1)
    x1 = x_ref[...] + jnp.dot(mixed, wout_ref[...], preferred_element_type=F32)
    hq = _rms_rows(x1, gc_ref[...]).astype(BF16)
    cq = jnp.dot(hq, wq_ref[...], preferred_element_type=F32)
    scale = HEAD_DIM ** -0.5
    outs = []
    for h in range(CROSS_HEADS):
        sl = slice(h * HEAD_DIM, (h + 1) * HEAD_DIM)
        qh = _rms_rows(cq[:, sl], gq_ref[...]).astype(BF16)
        sc = _dot_t(qh, ck_ref[:, sl]) * scale
        m = jnp.max(sc, axis=-1, keepdims=True)
        p = jnp.exp(sc - m)
        l = jnp.sum(p, axis=-1, keepdims=True)
        outs.append(jnp.dot(p.astype(BF16), cv_ref[:, sl], preferred_element_type=F32) / l)
    co = jnp.concatenate(outs, axis=-1).astype(BF16)
    o_ref[...] = x1 + jnp.dot(co, wo_ref[...], preferred_element_type=F32)


def _mid(x, om, od, osa, wout, gc, wq, gq, ck, cv, wo, s):
    tm = min(s, 512)
    n_mem = ck.shape[0]
    whole = lambda shape: pl.BlockSpec(shape, lambda i: (0,) * len(shape), pipeline_mode=pl.Buffered(1))
    heads = lambda n: pl.BlockSpec((n, tm, LANES), lambda i: (0, i, 0))
    return pl.pallas_call(
        _mid_kernel,
        grid=(s // tm,),
        in_specs=[
            pl.BlockSpec((tm, D_MODEL), lambda i: (i, 0)),
            heads(MOBA_HEADS), heads(DIFF_HEADS), heads(DSA_HEADS),
            whole((D_MODEL, D_MODEL)), whole((1, D_MODEL)), whole((D_MODEL, CROSS_W)),
            whole((1, LANES)), whole((n_mem, CROSS_W)), whole((n_mem, CROSS_W)),
            whole((CROSS_W, D_MODEL)),
        ],
        out_specs=pl.BlockSpec((tm, D_MODEL), lambda i: (i, 0)),
        out_shape=jax.ShapeDtypeStruct((s, D_MODEL), F32),
        compiler_params=_cparams(("arbitrary",)),
        name="out_proj_cross",
    )(x, om, od, osa, wout, gc, wq, gq, ck, cv, wo)


FFN_HALO = 16
FFN_ROW_CHUNK = 256

def _ffn_kernel(x_ref, xp_ref, g_ref, wg_ref, wv_ref, cwg_ref, cwv_ref, cbg_ref, cbv_ref,
                wd_ref, o_ref, h_scr, *, tm):
    i = pl.program_id(0)
    f = pl.program_id(1)

    @pl.when(f == 0)
    def _():
        prev = _rms_rows(xp_ref[...], g_ref[...])
        h_scr[0:FFN_HALO, :] = jnp.where(i > 0, prev, 0.0).astype(BF16)
        h_scr[FFN_HALO:, :] = _rms_rows(x_ref[...], g_ref[...]).astype(BF16)
        o_ref[...] = x_ref[...]

    for r0 in range(0, tm, FFN_ROW_CHUNK):
        h = h_scr[r0:r0 + FFN_ROW_CHUNK + FFN_HALO, :]

        def conv(w_ref, cw_ref, cb_ref):
            u = jnp.dot(h, w_ref[...], preferred_element_type=F32)
            cw = cw_ref[...]
            uc = (cw[0:1] * pltpu.roll(u, 2, 0) + cw[1:2] * pltpu.roll(u, 1, 0) + cw[2:3] * u
                  + cb_ref[...])
            return uc[FFN_HALO:, :]

        gate = conv(wg_ref, cwg_ref, cbg_ref)
        val = conv(wv_ref, cwv_ref, cbv_ref)
        act = (gate * jax.nn.sigmoid(gate) * val).astype(BF16)
        o_ref[r0:r0 + FFN_ROW_CHUNK, :] += jnp.dot(act, wd_ref[...], preferred_element_type=F32)


def _ffn(x, g, w_up_p, cw_p, cb_p, w_down_p, s):
    tm = min(s, 512)
    nf = D_FF_PAD // FF_TILE
    halo_blocks = tm // FFN_HALO
    return pl.pallas_call(
        functools.partial(_ffn_kernel, tm=tm),
        grid=(s // tm, nf),
        in_specs=[
            pl.BlockSpec((tm, D_MODEL), lambda i, f: (i, 0)),
            pl.BlockSpec((FFN_HALO, D_MODEL), lambda i, f: (jnp.maximum(i * halo_blocks - 1, 0), 0)),
            pl.BlockSpec((1, D_MODEL), lambda i, f: (0, 0)),
            pl.BlockSpec((D_MODEL, FF_TILE), lambda i, f: (0, f)),
            pl.BlockSpec((D_MODEL, FF_TILE), lambda i, f: (0, f + nf)),
            pl.BlockSpec((3, FF_TILE), lambda i, f: (0, f)),
            pl.BlockSpec((3, FF_TILE), lambda i, f: (0, f + nf)),
            pl.BlockSpec((1, FF_TILE), lambda i, f: (0, f)),
            pl.BlockSpec((1, FF_TILE), lambda i, f: (0, f + nf)),
            pl.BlockSpec((FF_TILE, D_MODEL), lambda i, f: (f, 0)),
        ],
        out_specs=pl.BlockSpec((tm, D_MODEL), lambda i, f: (i, 0)),
        out_shape=jax.ShapeDtypeStruct((s, D_MODEL), F32),
        scratch_shapes=[pltpu.VMEM((tm + FFN_HALO, D_MODEL), BF16)],
        compiler_params=_cparams(("arbitrary", "arbitrary")),
        name="ffn",
    )(x, x, g, w_up_p, w_up_p, cw_p, cw_p, cb_p, cb_p, w_down_p)


CAST_ROWS = 128


def _cast_kernel(w_ref, o_ref, *, rows_in):
    live = pl.program_id(0) * CAST_ROWS < rows_in
    o_ref[...] = jnp.where(live, w_ref[0], 0.0).astype(BF16)


def _cast_layer(w, layer, rows_out=None):
    _, rows, n = w.shape
    rows_out = rows if rows_out is None else rows_out
    last = rows // CAST_ROWS - 1
    return pl.pallas_call(
        functools.partial(_cast_kernel, rows_in=rows),
        grid=(rows_out // CAST_ROWS,),
        in_specs=[pl.BlockSpec((1, CAST_ROWS, n), lambda i: (layer, jnp.minimum(i, last), 0))],
        out_specs=pl.BlockSpec((CAST_ROWS, n), lambda i: (i, 0)),
        out_shape=jax.ShapeDtypeStruct((rows_out, n), BF16),
        compiler_params=_cparams(("arbitrary",)),
        name="cast_weight",
    )(w)


def _cast_up_kernel(w_ref, o_ref):
    w = w_ref[0]
    pad = jnp.zeros((w.shape[0], D_FF_PAD - D_FF), BF16)
    o_ref[:, 0:D_FF] = w[:, 0:D_FF].astype(BF16)
    o_ref[:, D_FF:D_FF_PAD] = pad
    o_ref[:, D_FF_PAD:D_FF_PAD + D_FF] = w[:, D_FF:].astype(BF16)
    o_ref[:, D_FF_PAD + D_FF:] = pad


def _cast_up(w_up, layer):
    return pl.pallas_call(
        _cast_up_kernel,
        grid=(D_MODEL // CAST_ROWS,),
        in_specs=[pl.BlockSpec((1, CAST_ROWS, 2 * D_FF), lambda i: (layer, i, 0))],
        out_specs=pl.BlockSpec((CAST_ROWS, 2 * D_FF_PAD), lambda i: (i, 0)),
        out_shape=jax.ShapeDtypeStruct((D_MODEL, 2 * D_FF_PAD), BF16),
        compiler_params=_cparams(("arbitrary",)),
        name="cast_w_up",
    )(w_up)


def _split_pad(a):
    r = a.shape[0]
    halves = jnp.pad(a.reshape(r, 2, D_FF), ((0, 0), (0, 0), (0, D_FF_PAD - D_FF)))
    return halves.reshape(r, 2 * D_FF_PAD)


def kernel(x, mem, positions, attn_norm, w_in, moba_qk_gain, diff_qk_gain, diff_lambda,
           diff_subln, dsa_qk_gain, w_out, cross_norm, mem_norm, cross_wq, cross_wkv,
           cross_qk_gain, cross_wo, ffn_norm, ffn_w_up, ffn_conv_w, ffn_conv_b, ffn_w_down):
    b, s, _ = x.shape
    assert b == 1 and s % DSA_TQ == 0 and s % MOBA_BLOCK == 0
    xs = x.reshape(s, D_MODEL)
    mem2d = mem.reshape(mem.shape[1], D_MODEL)
    tabs = _rope_tables(positions, s)
    row = lambda v: v.reshape(1, -1)
    for l in range(DEPTH):
        gains = jnp.concatenate([
            moba_qk_gain[l], dsa_qk_gain[l], jnp.tile(diff_qk_gain[l], (1, 2)),
            jnp.zeros((2, LANES), F32)], axis=0)
        main, tail = _in_proj(xs, row(attn_norm[l]), w_in, l, tabs, gains, s)
        o_moba = _moba(main, s)
        o_diff = _diff(main, diff_lambda[l], row(diff_subln[l]), l, s)
        o_dsa = _dsa(main, tail, s)
        ck, cv = _memkv(mem2d, row(mem_norm[l]), _cast_layer(cross_wkv, l), row(cross_qk_gain[l, 1]))
        xs = _mid(xs, o_moba, o_diff, o_dsa, _cast_layer(w_out, l), row(cross_norm[l]),
                  _cast_layer(cross_wq, l), row(cross_qk_gain[l, 0]), ck, cv,
                  _cast_layer(cross_wo, l), s)
        w_up_p = _cast_up(ffn_w_up, l)
        cw_p = _split_pad(ffn_conv_w[l])
        cb_p = _split_pad(row(ffn_conv_b[l]))
        w_down_p = _cast_layer(ffn_w_down, l, D_FF_PAD)
        xs = _ffn(xs, row(ffn_norm[l]), w_up_p, cw_p, cb_p, w_down_p, s)
    return xs.reshape(b, s, D_MODEL)
```
